```python
import jax, jax.numpy as jnp
from jax import lax
import numpy as np

D_MODEL = 1024
BATCH = 8
SEQ = 4096
DEPTH = 4

N_META = 16
CHUNK = 64
META_PAD = (-N_META) % CHUNK
GLA_H = 4
GLA_DK = D_MODEL // 2 // GLA_H
GLA_DV = D_MODEL // GLA_H
GLA_LR = 16
GLA_TAU = 16.0
ML_H = 4
ML_DK = D_MODEL // 2 // ML_H
ML_DV = D_MODEL // ML_H
ML_CONV = 4
ML_F_BIAS = 3.0
D_FF = ((8 * D_MODEL // 3) + 127) // 128 * 128
FFN_CONV = 3
EPS = 1e-6

GLA_QK = GLA_H * GLA_DK
GLA_V = GLA_H * GLA_DV
ML_QK = ML_H * ML_DK
ML_V = ML_H * ML_DV
IN_SIZES = (GLA_QK, GLA_QK, GLA_V, GLA_V, GLA_LR,
            ML_QK, ML_QK, ML_V, ML_V, ML_H, ML_H,
            D_MODEL, D_MODEL)
N_IN = sum(IN_SIZES)

kernel_name = 'hybrid_gla_mlstm_convffn_meta'


def rms_norm(x, g):
    xf = x.astype(jnp.float32)
    y = xf * lax.rsqrt(jnp.mean(xf * xf, axis=-1, keepdims=True) + EPS)
    return (y * g.astype(jnp.float32)).astype(x.dtype)


def causal_dwconv(x, w, b):
    k = w.shape[0]
    y = lax.conv_general_dilated(x, w[:, None, :].astype(x.dtype), window_strides=(1,),
                                 padding=[(k - 1, 0)], dimension_numbers=('NWC', 'WIO', 'NWC'),
                                 feature_group_count=x.shape[-1])
    return y + b.astype(x.dtype)


def to_chunks(x, n_heads):
    b, l, hd = x.shape
    x = jnp.pad(x, ((0, 0), (META_PAD, 0), (0, 0)))
    n = x.shape[1] // CHUNK
    return x.reshape(b, n, CHUNK, n_heads, hd // n_heads).transpose(1, 0, 3, 2, 4)


def from_chunks(y):
    n, b, h, c, d = y.shape
    return y.transpose(1, 0, 3, 2, 4).reshape(b, n * c, h * d)[:, META_PAD:, :]


def head_rms(y, g):
    return y * lax.rsqrt(jnp.mean(y * y, axis=-1, keepdims=True) + EPS) * g.astype(jnp.float32)


def gla_chunked(q, k, v, log_a):
    causal = jnp.tril(jnp.ones((CHUNK, CHUNK), dtype=bool))

    def step(S, inp):
        qc, kc, vc, la = inp
        bc = jnp.cumsum(la, axis=-2)
        inter = jnp.einsum('bhtk,bhkv->bhtv', qc * jnp.exp(bc), S)
        diff = bc[:, :, :, None, :] - bc[:, :, None, :, :]
        decay = jnp.exp(jnp.where(causal[:, :, None], diff, -jnp.inf))
        scores = jnp.einsum('bhtk,bhsk,bhtsk->bhts', qc, kc, decay)
        out = inter + jnp.einsum('bhts,bhsv->bhtv', scores, vc)
        b_last = bc[:, :, -1:, :]
        S_new = jnp.exp(b_last[:, :, 0, :])[..., None] * S + jnp.einsum(
            'bhsk,bhsv->bhkv', kc * jnp.exp(b_last - bc), vc)
        return S_new, out

    n, b, h, c, dk = q.shape
    S0 = jnp.zeros((b, h, dk, v.shape[-1]), jnp.float32)
    _, out = lax.scan(step, S0, (q, k, v, log_a))
    return out


def mlstm_chunked(q, k, v, log_i, log_f):
    causal = jnp.tril(jnp.ones((CHUNK, CHUNK), dtype=bool))

    def step(carry, inp):
        C_prev, n_prev, m_prev = carry
        qc, kc, vc, li, lf = inp
        F = jnp.cumsum(lf, axis=-1)
        d_intra = jnp.where(causal, F[..., :, None] - F[..., None, :] + li[..., None, :], -jnp.inf)
        d_inter = F + m_prev[..., None]
        m = jnp.maximum(d_inter, jnp.max(d_intra, axis=-1))
        w_intra = jnp.exp(d_intra - m[..., None])
        w_inter = jnp.exp(d_inter - m)
        qk = jnp.einsum('bhtk,bhsk->bhts', qc, kc) * w_intra
        num = w_inter[..., None] * jnp.einsum('bhtk,bhkv->bhtv', qc, C_prev) + jnp.einsum('bhts,bhsv->bhtv', qk, vc)
        den = w_inter * jnp.einsum('bhtk,bhk->bht', qc, n_prev) + jnp.sum(qk, axis=-1)
        h = num / jnp.maximum(jnp.abs(den), jnp.exp(-m))[..., None]
        F_last = F[..., -1]
        d_state = F_last[..., None] - F + li
        m_new = jnp.maximum(F_last + m_prev, jnp.max(d_state, axis=-1))
        w_state = jnp.exp(d_state - m_new[..., None])
        g_prev = jnp.exp(F_last + m_prev - m_new)
        C_new = g_prev[..., None, None] * C_prev + jnp.einsum('bhs,bhsk,bhsv->bhkv', w_state, kc, vc)
        n_new = g_prev[..., None] * n_prev + jnp.einsum('bhs,bhsk->bhk', w_state, kc)
        return (C_new, n_new, m_new), h

    n, b, hh, c, dk = q.shape
    init = (jnp.zeros((b, hh, dk, v.shape[-1]), jnp.float32),
            jnp.zeros((b, hh, dk), jnp.float32),
            jnp.zeros((b, hh), jnp.float32))
    _, out = lax.scan(step, init, (q, k, v, log_i, log_f))
    return out


def mixer_block(h, w_in, b_in, gla_w_gate, gla_b_gate, gla_norm, ml_conv_w, ml_conv_b, ml_norm,
                w_branch_gla, w_branch_ml, w_out):
    f32 = jnp.float32
    proj = h @ w_in + b_in.astype(h.dtype)
    splits = [int(s) for s in np.cumsum(IN_SIZES)[:-1]]
    (g_q, g_k, g_v, g_r, g_lr, m_q, m_k, m_v, m_o, m_i, m_f, gate_a, gate_b) = jnp.split(proj, splits, axis=-1)

    log_a = jax.nn.log_sigmoid((g_lr @ gla_w_gate + gla_b_gate.astype(h.dtype)).astype(f32)) / GLA_TAU
    o_gla = gla_chunked(to_chunks(g_q.astype(f32) * (GLA_DK ** -0.5), GLA_H),
                        to_chunks(g_k.astype(f32), GLA_H),
                        to_chunks(g_v.astype(f32), GLA_H),
                        to_chunks(log_a, GLA_H))
    o_gla = from_chunks(head_rms(o_gla, gla_norm)).astype(h.dtype) * jax.nn.silu(g_r)

    qk = jax.nn.silu(causal_dwconv(jnp.concatenate([m_q, m_k], axis=-1), ml_conv_w, ml_conv_b))
    mq, mk = jnp.split(qk, [ML_QK], axis=-1)
    o_ml = mlstm_chunked(to_chunks(mq.astype(f32) * (ML_DK ** -0.5), ML_H),
                         to_chunks(mk.astype(f32), ML_H),
                         to_chunks(m_v.astype(f32), ML_H),
                         to_chunks(m_i.astype(f32), ML_H)[..., 0],
                         to_chunks(jax.nn.log_sigmoid(m_f.astype(f32)), ML_H)[..., 0])
    o_ml = from_chunks(head_rms(o_ml, ml_norm.reshape(ML_H, 1, ML_DV))).astype(h.dtype) * jax.nn.sigmoid(m_o)

    y = jax.nn.sigmoid(gate_a) * (o_gla @ w_branch_gla) + jax.nn.sigmoid(gate_b) * (o_ml @ w_branch_ml)
    return y @ w_out


def conv_ffn(h, w_up, conv_w, conv_b, w_down):
    u = causal_dwconv(h @ w_up, conv_w, conv_b)
    gate, up = jnp.split(u, [D_FF], axis=-1)
    return (jax.nn.silu(gate) * up) @ w_down


def setup_inputs(seed: int = 0) -> dict:
    key = jax.random.key(seed)
    ks = jax.random.split(key, 24)
    nrm = lambda k, shape, s: jax.random.normal(k, shape, jnp.float32) * s
    f_off = sum(IN_SIZES[:10])
    b_in = nrm(ks[4], (DEPTH, N_IN), 0.02)
    b_in = b_in.at[:, f_off:f_off + ML_H].add(ML_F_BIAS)
    return {
        'x': jax.random.normal(ks[0], (BATCH, SEQ, D_MODEL), jnp.float32),
        'meta': nrm(ks[1], (N_META, D_MODEL), 1.0),
        'norm_mix': 1.0 + nrm(ks[2], (DEPTH, D_MODEL), 0.02),
        'w_in': nrm(ks[3], (DEPTH, D_MODEL, N_IN), D_MODEL ** -0.5),
        'b_in': b_in,
        'gla_w_gate': nrm(ks[5], (DEPTH, GLA_LR, GLA_QK), GLA_LR ** -0.5),
        'gla_b_gate': nrm(ks[6], (DEPTH, GLA_QK), 0.02),
        'gla_norm': 1.0 + nrm(ks[7], (DEPTH, GLA_DV), 0.02),
        'ml_conv_w': nrm(ks[8], (DEPTH, ML_CONV, 2 * ML_QK), ML_CONV ** -0.5),
        'ml_conv_b': nrm(ks[9], (DEPTH, 2 * ML_QK), 0.02),
        'ml_norm': 1.0 + nrm(ks[10], (DEPTH, ML_V), 0.02),
        'w_branch_gla': nrm(ks[11], (DEPTH, GLA_V, D_MODEL), GLA_V ** -0.5),
        'w_branch_ml': nrm(ks[12], (DEPTH, ML_V, D_MODEL), ML_V ** -0.5),
        'w_out': nrm(ks[13], (DEPTH, D_MODEL, D_MODEL), D_MODEL ** -0.5),
        'norm_ffn': 1.0 + nrm(ks[14], (DEPTH, D_MODEL), 0.02),
        'ffn_w_up': nrm(ks[15], (DEPTH, D_MODEL, 2 * D_FF), D_MODEL ** -0.5),
        'ffn_conv_w': nrm(ks[16], (DEPTH, FFN_CONV, 2 * D_FF), FFN_CONV ** -0.5),
        'ffn_conv_b': nrm(ks[17], (DEPTH, 2 * D_FF), 0.02),
        'ffn_w_down': nrm(ks[18], (DEPTH, D_FF, D_MODEL), D_FF ** -0.5),
        'norm_final': 1.0 + nrm(ks[19], (D_MODEL,), 0.02),
    }


def reference(x, meta, norm_mix, w_in, b_in, gla_w_gate, gla_b_gate, gla_norm, ml_conv_w, ml_conv_b,
              ml_norm, w_branch_gla, w_branch_ml, w_out, norm_ffn, ffn_w_up, ffn_conv_w, ffn_conv_b,
              ffn_w_down, norm_final):
    b = x.shape[0]
    meta_tok = jnp.broadcast_to(meta[None].astype(x.dtype), (b, N_META, D_MODEL))
    h = jnp.concatenate([meta_tok, x], axis=1)
    for l in range(DEPTH):
        h = h + mixer_block(rms_norm(h, norm_mix[l]), w_in[l], b_in[l], gla_w_gate[l], gla_b_gate[l],
                            gla_norm[l], ml_conv_w[l], ml_conv_b[l], ml_norm[l],
                            w_branch_gla[l], w_branch_ml[l], w_out[l])
        h = h + conv_ffn(rms_norm(h, norm_ffn[l]), ffn_w_up[l], ffn_conv_w[l], ffn_conv_b[l], ffn_w_down[l])
    h = rms_norm(h, norm_final)
    return h[:, N_META:, :]
```

```python
import functools

import jax
import jax.numpy as jnp
from jax import lax
from jax.experimental import pallas as pl
from jax.experimental.pallas import tpu as pltpu

F32 = jnp.float32
BF16 = jnp.bfloat16

D_MODEL = 1024
N_META = 16
CHUNK = 64
GLA_H = 4
GLA_DK = 128
GLA_DV = 256
GLA_LR = 16
GLA_TAU = 16.0
ML_H = 4
ML_DK = 128
ML_DV = 256
ML_CONV = 4
D_FF = 2816
FFN_CONV = 3
EPS = 1e-6

GLA_QK = GLA_H * GLA_DK
GLA_V = GLA_H * GLA_DV
ML_QK = ML_H * ML_DK
ML_V = ML_H * ML_DV

TM = 832
CHUNKS_PER_TILE = TM // CHUNK
N_BIG = 8192
N_SMALL = 128
SMALL_LR = 0
SMALL_I = 16
SMALL_F = 20
INPROJ_TN = 2048
FF_TILE = 256
N_FF_TILES = D_FF // FF_TILE
HALO = 16
VMEM_LIMIT = 56 * 1024 * 1024

NT_DIMS = (((1,), (1,)), ((), ()))
TN_DIMS = (((0,), (0,)), ((), ()))


def _rms(x, g):
    return x * lax.rsqrt(jnp.mean(x * x, axis=-1, keepdims=True) + EPS) * g


def _log_sigmoid(x):
    return jnp.minimum(x, 0.0) - jnp.log1p(jnp.exp(-jnp.abs(x)))


def _sigmoid(x):
    return 1.0 / (1.0 + jnp.exp(-x))


def _tri(n):
    r = lax.broadcasted_iota(jnp.int32, (n, n), 0)
    c = lax.broadcasted_iota(jnp.int32, (n, n), 1)
    return r >= c


def _params(sem):
    return pltpu.CompilerParams(dimension_semantics=sem, vmem_limit_bytes=VMEM_LIMIT)


def _inproj_kernel(h_ref, g_ref, w_ref, b_ref, ws_ref, bs_ref, big_ref, small_ref, xn_ref):
    @pl.when(pl.program_id(1) == 0)
    def _():
        xn = _rms(h_ref[0], g_ref[...]).astype(BF16)
        xn_ref[...] = xn
        small_ref[0] = jnp.dot(xn, ws_ref[...], preferred_element_type=F32) + bs_ref[...]

    acc = jnp.dot(xn_ref[...], w_ref[...], preferred_element_type=F32)
    big_ref[0] = (acc + b_ref[...]).astype(BF16)


def _inproj(h, g, w, b, ws, bs):
    r = h.shape[0]
    return pl.pallas_call(
        _inproj_kernel,
        grid=(r, N_BIG // INPROJ_TN),
        in_specs=[
            pl.BlockSpec((1, TM, D_MODEL), lambda i, j: (i, 0, 0)),
            pl.BlockSpec((1, D_MODEL), lambda i, j: (0, 0)),
            pl.BlockSpec((D_MODEL, INPROJ_TN), lambda i, j: (0, j)),
            pl.BlockSpec((1, INPROJ_TN), lambda i, j: (0, j)),
            pl.BlockSpec((D_MODEL, N_SMALL), lambda i, j: (0, 0)),
            pl.BlockSpec((1, N_SMALL), lambda i, j: (0, 0)),
        ],
        out_specs=[
            pl.BlockSpec((1, TM, INPROJ_TN), lambda i, j: (i, 0, j)),
            pl.BlockSpec((1, TM, N_SMALL), lambda i, j: (i, 0, 0)),
        ],
        out_shape=[
            jax.ShapeDtypeStruct((r, TM, N_BIG), BF16),
            jax.ShapeDtypeStruct((r, TM, N_SMALL), F32),
        ],
        scratch_shapes=[pltpu.VMEM((TM, D_MODEL), BF16)],
        compiler_params=_params(("arbitrary", "arbitrary")),
        name="inproj",
    )(h, g, w, b, ws, bs)


def _gla_kernel(tiles_per_batch, q_ref, k_ref, v_ref, r_ref, small_ref, wg_ref, bg_ref, gn_ref,
                o_ref, s_ref):
    @pl.when(pl.program_id(0) % tiles_per_batch == 0)
    def _():
        s_ref[...] = jnp.zeros_like(s_ref)

    tri = _tri(CHUNK)
    tri_f = tri.astype(F32)
    half = CHUNK // 2

    def chunk(c, carry):
        rows = pl.ds(pl.multiple_of(c * CHUNK, CHUNK), CHUNK)
        sm = small_ref[0, rows, :]
        x = jnp.dot(sm.astype(BF16), wg_ref[...], preferred_element_type=F32) + bg_ref[...]
        la = _log_sigmoid(x) * (1.0 / GLA_TAU)
        bc = jnp.dot(tri_f, la, precision=lax.Precision.HIGHEST, preferred_element_type=F32)
        ref = bc[half - 1:half, :]
        bend = bc[CHUNK - 1:CHUNK, :]
        e_fwd = jnp.exp(bc - ref)
        e_bwd = jnp.exp(ref - bc)
        q = q_ref[0, rows, :].astype(F32) * (GLA_DK ** -0.5)
        k = k_ref[0, rows, :].astype(F32)
        qd = q * e_fwd
        qs = qd * jnp.exp(ref)
        kd = k * e_bwd
        ks = kd * jnp.exp(bend - ref)
        e_end = jnp.exp(bend)
        for h in range(GLA_H):
            ksl = slice(h * GLA_DK, (h + 1) * GLA_DK)
            vsl = slice(h * GLA_DV, (h + 1) * GLA_DV)
            st = s_ref[h]
            v = v_ref[0, rows, vsl]
            inter = lax.dot_general(qs[:, ksl].astype(BF16), st.astype(BF16), NT_DIMS,
                                    preferred_element_type=F32)
            sc = lax.dot_general(qd[:, ksl].astype(BF16), kd[:, ksl].astype(BF16), NT_DIMS,
                                 preferred_element_type=F32)
            sc = jnp.where(tri, sc, 0.0)
            o = inter + jnp.dot(sc.astype(BF16), v, preferred_element_type=F32)
            s_ref[h] = st * e_end[:, ksl] + lax.dot_general(
                v, ks[:, ksl].astype(BF16), TN_DIMS, preferred_element_type=F32)
            o = _rms(o, gn_ref[...])
            r = r_ref[0, rows, vsl].astype(F32)
            o_ref[0, rows, vsl] = (o * (r * _sigmoid(r))).astype(BF16)
        return carry

    lax.fori_loop(0, CHUNKS_PER_TILE, chunk, 0)


def _gla(big, small, wg, bg, gn, tiles_per_batch):
    r = big.shape[0]
    return pl.pallas_call(
        functools.partial(_gla_kernel, tiles_per_batch),
        grid=(r,),
        in_specs=[
            pl.BlockSpec((1, TM, GLA_QK), lambda i: (i, 0, 0)),
            pl.BlockSpec((1, TM, GLA_QK), lambda i: (i, 0, 1)),
            pl.BlockSpec((1, TM, GLA_V), lambda i: (i, 0, 1)),
            pl.BlockSpec((1, TM, GLA_V), lambda i: (i, 0, 2)),
            pl.BlockSpec((1, TM, N_SMALL), lambda i: (i, 0, 0)),
            pl.BlockSpec((N_SMALL, GLA_QK), lambda i: (0, 0)),
            pl.BlockSpec((1, GLA_QK), lambda i: (0, 0)),
            pl.BlockSpec((1, GLA_DV), lambda i: (0, 0)),
        ],
        out_specs=pl.BlockSpec((1, TM, GLA_V), lambda i: (i, 0, 0)),
        out_shape=jax.ShapeDtypeStruct((r, TM, GLA_V), BF16),
        scratch_shapes=[pltpu.VMEM((GLA_H, GLA_DV, GLA_DK), F32)],
        compiler_params=_params(("arbitrary",)),
        name="gla",
    )(big, big, big, big, small, wg, bg, gn)


def _mlstm_kernel(tiles_per_batch, q_ref, k_ref, v_ref, o_gate_ref, small_ref, cw_ref, cb_ref,
                  mn_ref, o_ref, xin_ref, c_ref, n_ref, m_ref):
    first = pl.program_id(0) % tiles_per_batch == 0

    @pl.when(first)
    def _():
        c_ref[...] = jnp.zeros_like(c_ref)
        n_ref[...] = jnp.zeros_like(n_ref)
        m_ref[...] = jnp.zeros_like(m_ref)
        xin_ref[0:8, :] = jnp.zeros((8, 2 * ML_QK), F32)

    @pl.when(jnp.logical_not(first))
    def _():
        xin_ref[0:8, :] = xin_ref[TM:TM + 8, :]

    xin_ref[8:TM + 8, 0:ML_QK] = q_ref[0].astype(F32)
    xin_ref[8:TM + 8, ML_QK:2 * ML_QK] = k_ref[0].astype(F32)

    tri = _tri(CHUNK)
    tri_f = tri.astype(F32)
    cw = cw_ref[...]
    cb = cb_ref[...]

    def chunk(c, carry):
        r0 = pl.multiple_of(c * CHUNK, CHUNK)
        rows = pl.ds(r0, CHUNK)
        win = xin_ref[pl.ds(r0, CHUNK + 8), :]
        acc = cb + cw[ML_CONV - 1:ML_CONV, :] * win[8:8 + CHUNK, :]
        for j in range(ML_CONV - 1):
            shift = ML_CONV - 1 - j
            acc = acc + cw[j:j + 1, :] * win[8 - shift:8 - shift + CHUNK, :]
        qk = acc * _sigmoid(acc)
        qc = (qk[:, :ML_QK] * (ML_DK ** -0.5)).astype(BF16)
        kc = qk[:, ML_QK:]
        sm = small_ref[0, rows, :]
        fc = jnp.dot(tri_f, _log_sigmoid(sm), precision=lax.Precision.HIGHEST,
                     preferred_element_type=F32)
        fct = fc.T
        smt = sm.T
        for h in range(ML_H):
            ksl = slice(h * ML_DK, (h + 1) * ML_DK)
            vsl = slice(h * ML_DV, (h + 1) * ML_DV)
            f_col = fc[:, SMALL_F + h:SMALL_F + h + 1]
            f_row = fct[SMALL_F + h:SMALL_F + h + 1, :]
            li_col = sm[:, SMALL_I + h:SMALL_I + h + 1]
            li_row = smt[SMALL_I + h:SMALL_I + h + 1, :]
            m_prev = m_ref[h, 0:1, 0:1]
            d_intra = jnp.where(tri, f_col - f_row + li_row, -jnp.inf)
            d_inter = f_col + m_prev
            m = jnp.maximum(d_inter, jnp.max(d_intra, axis=-1, keepdims=True))
            w_intra = jnp.exp(d_intra - m)
            w_inter = jnp.exp(d_inter - m)
            qh = qc[:, ksl]
            kh = kc[:, ksl]
            v = v_ref[0, rows, vsl]
            ct = c_ref[h]
            n_prev = n_ref[h, 0:1, :]
            qkw = lax.dot_general(qh, kh.astype(BF16), NT_DIMS, preferred_element_type=F32) * w_intra
            num = w_inter * lax.dot_general(qh, ct.astype(BF16), NT_DIMS, preferred_element_type=F32)
            num = num + jnp.dot(qkw.astype(BF16), v, preferred_element_type=F32)
            den = w_inter * jnp.sum(qh.astype(F32) * n_prev, axis=-1, keepdims=True)
            den = den + jnp.sum(qkw, axis=-1, keepdims=True)
            hh = num / jnp.maximum(jnp.abs(den), jnp.exp(-m))
            f_last = f_col[CHUNK - 1:CHUNK, :]
            d_state = f_last - f_col + li_col
            m_new = jnp.maximum(f_last + m_prev, jnp.max(d_state, axis=0, keepdims=True))
            w_state = jnp.exp(d_state - m_new)
            g_prev = jnp.exp(f_last + m_prev - m_new)
            kw = kh * w_state
            c_ref[h] = g_prev * ct + lax.dot_general(v, kw.astype(BF16), TN_DIMS,
                                                     preferred_element_type=F32)
            n_ref[h] = jnp.broadcast_to(g_prev * n_prev + jnp.sum(kw, axis=0, keepdims=True),
                                        (8, ML_DK))
            m_ref[h] = jnp.broadcast_to(m_new, (8, 128))
            hh = _rms(hh, mn_ref[:, vsl])
            og = o_gate_ref[0, rows, vsl].astype(F32)
            o_ref[0, rows, vsl] = (hh * _sigmoid(og)).astype(BF16)
        return carry

    lax.fori_loop(0, CHUNKS_PER_TILE, chunk, 0)


def _mlstm(big, small, cw, cb, mn, tiles_per_batch):
    r = big.shape[0]
    return pl.pallas_call(
        functools.partial(_mlstm_kernel, tiles_per_batch),
        grid=(r,),
        in_specs=[
            pl.BlockSpec((1, TM, ML_QK), lambda i: (i, 0, 6)),
            pl.BlockSpec((1, TM, ML_QK), lambda i: (i, 0, 7)),
            pl.BlockSpec((1, TM, ML_V), lambda i: (i, 0, 4)),
            pl.BlockSpec((1, TM, ML_V), lambda i: (i, 0, 5)),
            pl.BlockSpec((1, TM, N_SMALL), lambda i: (i, 0, 0)),
            pl.BlockSpec((ML_CONV, 2 * ML_QK), lambda i: (0, 0)),
            pl.BlockSpec((1, 2 * ML_QK), lambda i: (0, 0)),
            pl.BlockSpec((1, ML_V), lambda i: (0, 0)),
        ],
        out_specs=pl.BlockSpec((1, TM, ML_V), lambda i: (i, 0, 0)),
        out_shape=jax.ShapeDtypeStruct((r, TM, ML_V), BF16),
        scratch_shapes=[
            pltpu.VMEM((TM + 8, 2 * ML_QK), F32),
            pltpu.VMEM((ML_H, ML_DV, ML_DK), F32),
            pltpu.VMEM((ML_H, 8, ML_DK), F32),
            pltpu.VMEM((ML_H, 8, 128), F32),
        ],
        compiler_params=_params(("arbitrary",)),
        name="mlstm",
    )(big, big, big, big, small, cw, cb, mn)


def _merge_kernel(og_ref, om_ref, ga_ref, gb_ref, h_ref, wg_ref, wm_ref, wo_ref, out_ref):
    a = jnp.dot(og_ref[0], wg_ref[...], preferred_element_type=F32)
    b = jnp.dot(om_ref[0], wm_ref[...], preferred_element_type=F32)
    y = _sigmoid(ga_ref[0].astype(F32)) * a + _sigmoid(gb_ref[0].astype(F32)) * b
    out_ref[0] = h_ref[0] + jnp.dot(y.astype(BF16), wo_ref[...], preferred_element_type=F32)


def _merge(o_gla, o_ml, big, h, wg, wm, wo):
    r = h.shape[0]
    wspec = pl.BlockSpec((D_MODEL, D_MODEL), lambda i: (0, 0))
    return pl.pallas_call(
        _merge_kernel,
        grid=(r,),
        in_specs=[
            pl.BlockSpec((1, TM, GLA_V), lambda i: (i, 0, 0)),
            pl.BlockSpec((1, TM, ML_V), lambda i: (i, 0, 0)),
            pl.BlockSpec((1, TM, D_MODEL), lambda i: (i, 0, 6)),
            pl.BlockSpec((1, TM, D_MODEL), lambda i: (i, 0, 7)),
            pl.BlockSpec((1, TM, D_MODEL), lambda i: (i, 0, 0)),
            wspec, wspec, wspec,
        ],
        out_specs=pl.BlockSpec((1, TM, D_MODEL), lambda i: (i, 0, 0)),
        out_shape=jax.ShapeDtypeStruct(h.shape, F32),
        input_output_aliases={4: 0},
        compiler_params=_params(("arbitrary",)),
        name="merge",
    )(o_gla, o_ml, big, big, h, wg, wm, wo)


def _ffn_kernel(tiles_per_batch, h_ref, halo_ref, g_ref, wup_ref, cw_ref, cb_ref, wdn_ref,
                out_ref, xn_ref, acc_ref):
    j = pl.program_id(1)

    @pl.when(j == 0)
    def _():
        keep = (pl.program_id(0) % tiles_per_batch != 0).astype(F32)
        xn_ref[0:HALO, :] = (_rms(halo_ref[0], g_ref[...]) * keep).astype(BF16)
        xn_ref[HALO:HALO + TM, :] = _rms(h_ref[0], g_ref[...]).astype(BF16)
        acc_ref[...] = jnp.zeros_like(acc_ref)

    u = jnp.dot(xn_ref[...], wup_ref[0], preferred_element_type=F32)
    cw = cw_ref[0]
    y = cb_ref[0] + cw[FFN_CONV - 1:FFN_CONV, :] * u[HALO:HALO + TM, :]
    for t in range(FFN_CONV - 1):
        shift = FFN_CONV - 1 - t
        y = y + cw[t:t + 1, :] * u[HALO - shift:HALO - shift + TM, :]
    gate = y[:, :FF_TILE]
    a = (gate * _sigmoid(gate) * y[:, FF_TILE:]).astype(BF16)
    acc_ref[...] += jnp.dot(a, wdn_ref[0], preferred_element_type=F32)

    @pl.when(j == N_FF_TILES - 1)
    def _():
        out_ref[0] = h_ref[0] + acc_ref[...]


def _ffn(h, g, wup, cw, cb, wdn, tiles_per_batch):
    r = h.shape[0]
    last_halo_block = TM // HALO - 1
    return pl.pallas_call(
        functools.partial(_ffn_kernel, tiles_per_batch),
        grid=(r, N_FF_TILES),
        in_specs=[
            pl.BlockSpec((1, TM, D_MODEL), lambda i, j: (i, 0, 0)),
            pl.BlockSpec((1, HALO, D_MODEL),
                         lambda i, j: (jnp.maximum(i - 1, 0), last_halo_block, 0)),
            pl.BlockSpec((1, D_MODEL), lambda i, j: (0, 0)),
            pl.BlockSpec((1, D_MODEL, 2 * FF_TILE), lambda i, j: (j, 0, 0)),
            pl.BlockSpec((1, FFN_CONV, 2 * FF_TILE), lambda i, j: (j, 0, 0)),
            pl.BlockSpec((1, 1, 2 * FF_TILE), lambda i, j: (j, 0, 0)),
            pl.BlockSpec((1, FF_TILE, D_MODEL), lambda i, j: (j, 0, 0)),
        ],
        out_specs=pl.BlockSpec((1, TM, D_MODEL), lambda i, j: (i, 0, 0)),
        out_shape=jax.ShapeDtypeStruct(h.shape, F32),
        scratch_shapes=[
            pltpu.VMEM((HALO + TM, D_MODEL), BF16),
            pltpu.VMEM((TM, D_MODEL), F32),
        ],
        compiler_params=_params(("arbitrary", "arbitrary")),
        name="ffn",
    )(h, h, g, wup, cw, cb, wdn)


def _final_kernel(a_ref, nxt_ref, g_ref, o_ref):
    y = jnp.concatenate([a_ref[0, N_META:, :], nxt_ref[0]], axis=0)
    o_ref[0] = _rms(y, g_ref[...])


def _final(h, g, seq):
    b, lp, _ = h.shape
    nxt_per_tile = TM // N_META
    last_nxt = lp // N_META - 1
    return pl.pallas_call(
        _final_kernel,
        grid=(b, lp // TM),
        in_specs=[
            pl.BlockSpec((1, TM, D_MODEL), lambda i, j: (i, j, 0)),
            pl.BlockSpec((1, N_META, D_MODEL),
                         lambda i, j: (i, jnp.minimum((j + 1) * nxt_per_tile, last_nxt), 0)),
            pl.BlockSpec((1, D_MODEL), lambda i, j: (0, 0)),
        ],
        out_specs=pl.BlockSpec((1, TM, D_MODEL), lambda i, j: (i, j, 0)),
        out_shape=jax.ShapeDtypeStruct((b, seq, D_MODEL), F32),
        compiler_params=_params(("arbitrary", "arbitrary")),
        name="final_norm",
    )(h, h, g)


def kernel(x, meta, norm_mix, w_in, b_in, gla_w_gate, gla_b_gate, gla_norm, ml_conv_w, ml_conv_b,
           ml_norm, w_branch_gla, w_branch_ml, w_out, norm_ffn, ffn_w_up, ffn_conv_w, ffn_conv_b,
           ffn_w_down, norm_final):
    b, seq, _ = x.shape
    depth = w_in.shape[0]
    lp = -(-(N_META + seq) // TM) * TM
    tiles_per_batch = lp // TM
    assert lp - (N_META + seq) >= 0 and lp % CHUNK == 0

    h = jnp.concatenate([
        jnp.broadcast_to(meta[None].astype(x.dtype), (b, N_META, D_MODEL)), x,
        jnp.zeros((b, lp - N_META - seq, D_MODEL), x.dtype)], axis=1)
    h = h.reshape(b * tiles_per_batch, TM, D_MODEL)

    c0 = 2 * GLA_QK + 2 * GLA_V
    c1 = c0 + GLA_LR
    c2 = c1 + 2 * ML_QK + 2 * ML_V
    c3 = c2 + 2 * ML_H
    w_big = jnp.concatenate([w_in[:, :, :c0], w_in[:, :, c1:c2], w_in[:, :, c3:]], axis=2).astype(BF16)
    b_big = jnp.concatenate([b_in[:, :c0], b_in[:, c1:c2], b_in[:, c3:]], axis=1)[:, None, :]
    n_small_used = GLA_LR + 2 * ML_H
    w_small = jnp.concatenate([w_in[:, :, c0:c1], w_in[:, :, c2:c3]], axis=2)
    w_small = jnp.pad(w_small, ((0, 0), (0, 0), (0, N_SMALL - n_small_used))).astype(BF16)
    b_small = jnp.concatenate([b_in[:, c0:c1], b_in[:, c2:c3]], axis=1)
    b_small = jnp.pad(b_small, ((0, 0), (0, N_SMALL - n_small_used)))[:, None, :]
    wg_gate = jnp.pad(gla_w_gate, ((0, 0), (0, N_SMALL - GLA_LR), (0, 0))).astype(BF16)
    wup = ffn_w_up.reshape(depth, D_MODEL, 2, N_FF_TILES, FF_TILE)
    wup = wup.transpose(0, 3, 1, 2, 4).reshape(depth, N_FF_TILES, D_MODEL, 2 * FF_TILE).astype(BF16)
    fcw = ffn_conv_w.reshape(depth, FFN_CONV, 2, N_FF_TILES, FF_TILE)
    fcw = fcw.transpose(0, 3, 1, 2, 4).reshape(depth, N_FF_TILES, FFN_CONV, 2 * FF_TILE)
    fcb = ffn_conv_b.reshape(depth, 2, N_FF_TILES, FF_TILE)
    fcb = fcb.transpose(0, 2, 1, 3).reshape(depth, N_FF_TILES, 1, 2 * FF_TILE)
    wdn = ffn_w_down.reshape(depth, N_FF_TILES, FF_TILE, D_MODEL).astype(BF16)
    wbg = w_branch_gla.astype(BF16)
    wbm = w_branch_ml.astype(BF16)
    wo = w_out.astype(BF16)

    for l in range(depth):
        big, small = _inproj(h, norm_mix[l][None], w_big[l], b_big[l], w_small[l], b_small[l])
        o_gla = _gla(big, small, wg_gate[l], gla_b_gate[l][None], gla_norm[l][None], tiles_per_batch)
        o_ml = _mlstm(big, small, ml_conv_w[l], ml_conv_b[l][None], ml_norm[l][None], tiles_per_batch)
        h = _merge(o_gla, o_ml, big, h, wbg[l], wbm[l], wo[l])
        h = _ffn(h, norm_ffn[l][None], wup[l], fcw[l], fcb[l], wdn[l], tiles_per_batch)

    return _final(h.reshape(b, lp, D_MODEL), norm_final[None], seq)
```

```python
import functools
import math

import jax
import jax.numpy as jnp
from jax import lax
from jax.experimental import pallas as pl
from jax.experimental.pallas import tpu as pltpu

F32 = jnp.float32
BF16 = jnp.bfloat16

D_MODEL = 1024
N_META = 16
CHUNK = 64
GLA_H = 4
GLA_DK = 128
GLA_DV = 256
GLA_LR = 16
GLA_TAU = 16.0
ML_H = 4
ML_DK = 128
ML_DV = 256
ML_CONV = 4
D_FF = 2816
FFN_CONV = 3
EPS = 1e-6
LOG2E = math.log2(math.e)

GLA_QK = GLA_H * GLA_DK
GLA_V = GLA_H * GLA_DV
ML_QK = ML_H * ML_DK
ML_V = ML_H * ML_DV

TM = 832
CHUNKS_PER_TILE = TM // CHUNK

INPROJ_TN = 2048
N_BIG = 4 * INPROJ_TN
OFF_GQ, OFF_GK, OFF_GV = 0, 512, 1024
OFF_MQ, OFF_MK, OFF_MV = 2048, 2560, 3072
OFF_GR, OFF_MO = 4096, 5120
OFF_GA, OFF_GB = 6144, 7168
N_SMALL = 128
SMALL_I = 16
SMALL_F = 20

FF_TILE = 256
N_FF_TILES = D_FF // FF_TILE
HALO = 16
VMEM_LIMIT = 56 * 1024 * 1024

NT_DIMS = (((1,), (1,)), ((), ()))
TN_DIMS = (((0,), (0,)), ((), ()))


def _rms(x, g):
    return x * lax.rsqrt(jnp.mean(x * x, axis=-1, keepdims=True) + EPS) * g


def _sigmoid(x):
    return 1.0 / (1.0 + jnp.exp2(x * (-LOG2E)))


def _log2_sigmoid(x):
    x2 = x * LOG2E
    return jnp.minimum(x2, 0.0) - jnp.log2(1.0 + jnp.exp2(-jnp.abs(x2)))


def _tri(n):
    r = lax.broadcasted_iota(jnp.int32, (n, n), 0)
    c = lax.broadcasted_iota(jnp.int32, (n, n), 1)
    return r >= c


def _cumsum_rows(tri_b, x):
    hi = x.astype(BF16)
    lo = (x - hi.astype(F32)).astype(BF16)
    return (jnp.dot(tri_b, hi, preferred_element_type=F32)
            + jnp.dot(tri_b, lo, preferred_element_type=F32))


def _params(sem):
    return pltpu.CompilerParams(dimension_semantics=sem, vmem_limit_bytes=VMEM_LIMIT)


def _skewed(n, phases):
    for i in range(n + len(phases) - 1):
        for k, phase in enumerate(phases):
            if 0 <= i - k < n:
                phase(i - k)


def _inproj_kernel(tiles_per_batch, h_ref, g_ref, w_ref, b_ref, ws_ref, bs_ref, wg_ref, bg_ref,
                   cw_ref, cb_ref, big_ref, la_ref, mg_ref, xn_ref, carry_ref):
    j = pl.program_id(1)

    def project():
        return jnp.dot(xn_ref[...], w_ref[...], preferred_element_type=F32) + b_ref[...]

    @pl.when(j == 0)
    def _():
        xn = _rms(h_ref[0], g_ref[...]).astype(BF16)
        xn_ref[...] = xn
        small = jnp.dot(xn, ws_ref[...], preferred_element_type=F32) + bs_ref[...]
        x = jnp.dot(small.astype(BF16), wg_ref[...], preferred_element_type=F32) + bg_ref[...]
        la_ref[0] = _log2_sigmoid(x) * (1.0 / GLA_TAU)
        lane = lax.broadcasted_iota(jnp.int32, small.shape, 1)
        is_f = jnp.logical_and(lane >= SMALL_F, lane < SMALL_F + ML_H)
        mg_ref[0] = jnp.where(is_f, _log2_sigmoid(small), small * LOG2E)
        acc = project()
        big_ref[0, :, :GLA_QK] = (acc[:, :GLA_QK] * (GLA_DK ** -0.5)).astype(BF16)
        big_ref[0, :, GLA_QK:] = acc[:, GLA_QK:].astype(BF16)

    @pl.when(j == 1)
    def _():
        @pl.when(pl.program_id(0) % tiles_per_batch == 0)
        def _():
            carry_ref[...] = jnp.zeros_like(carry_ref)

        acc = project()
        pre = acc[:, :2 * ML_QK]
        xe = jnp.concatenate([carry_ref[...], pre], axis=0)
        carry_ref[...] = pre[TM - 8:, :]
        y = cb_ref[...]
        for tap in range(ML_CONV):
            lo = 8 - (ML_CONV - 1) + tap
            y = y + cw_ref[tap:tap + 1, :] * xe[lo:lo + TM, :]
        qk = y * _sigmoid(y)
        big_ref[0, :, :ML_QK] = (qk[:, :ML_QK] * (ML_DK ** -0.5)).astype(BF16)
        big_ref[0, :, ML_QK:2 * ML_QK] = qk[:, ML_QK:].astype(BF16)
        big_ref[0, :, 2 * ML_QK:] = acc[:, 2 * ML_QK:].astype(BF16)

    @pl.when(j == 2)
    def _():
        acc = project()
        r = acc[:, :GLA_V]
        big_ref[0, :, :GLA_V] = (r * _sigmoid(r)).astype(BF16)
        big_ref[0, :, GLA_V:] = _sigmoid(acc[:, GLA_V:]).astype(BF16)

    @pl.when(j == 3)
    def _():
        big_ref[0] = _sigmoid(project()).astype(BF16)


def _inproj(h, g, w, b, ws, bs, wg, bg, cw, cb, tiles_per_batch):
    r = h.shape[0]
    const = lambda i, j: (0, 0)
    return pl.pallas_call(
        functools.partial(_inproj_kernel, tiles_per_batch),
        grid=(r, N_BIG // INPROJ_TN),
        in_specs=[
            pl.BlockSpec((1, TM, D_MODEL), lambda i, j: (i, 0, 0)),
            pl.BlockSpec((1, D_MODEL), const),
            pl.BlockSpec((D_MODEL, INPROJ_TN), lambda i, j: (0, j)),
            pl.BlockSpec((1, INPROJ_TN), lambda i, j: (0, j)),
            pl.BlockSpec((D_MODEL, N_SMALL), const),
            pl.BlockSpec((1, N_SMALL), const),
            pl.BlockSpec((N_SMALL, GLA_QK), const),
            pl.BlockSpec((1, GLA_QK), const),
            pl.BlockSpec((ML_CONV, 2 * ML_QK), const),
            pl.BlockSpec((1, 2 * ML_QK), const),
        ],
        out_specs=[
            pl.BlockSpec((1, TM, INPROJ_TN), lambda i, j: (i, 0, j)),
            pl.BlockSpec((1, TM, GLA_QK), lambda i, j: (i, 0, 0)),
            pl.BlockSpec((1, TM, N_SMALL), lambda i, j: (i, 0, 0)),
        ],
        out_shape=[
            jax.ShapeDtypeStruct((r, TM, N_BIG), BF16),
            jax.ShapeDtypeStruct((r, TM, GLA_QK), F32),
            jax.ShapeDtypeStruct((r, TM, N_SMALL), F32),
        ],
        scratch_shapes=[
            pltpu.VMEM((TM, D_MODEL), BF16),
            pltpu.VMEM((8, 2 * ML_QK), F32),
        ],
        compiler_params=_params(("arbitrary", "arbitrary")),
        name="inproj",
    )(h, g, w, b, ws, bs, wg, bg, cw, cb)


def _gla_kernel(tiles_per_batch, q_ref, k_ref, v_ref, r_ref, la_ref, gn_ref,
                o_ref, s_ref, qs_ref, qd_ref, kd_ref, ks_ref, eend_ref, oi_ref, kv_ref):
    @pl.when(pl.program_id(0) % tiles_per_batch == 0)
    def _():
        s_ref[...] = jnp.zeros_like(s_ref)

    tri = _tri(CHUNK)
    tri_b = tri.astype(BF16)
    half = CHUNK // 2

    def phase1(c):
        rows = slice(c * CHUNK, (c + 1) * CHUNK)
        bc = _cumsum_rows(tri_b, la_ref[0, rows, :])
        ref = bc[half - 1:half, :]
        bend = bc[CHUNK - 1:CHUNK, :]
        e_fwd = jnp.exp2(bc - ref)
        e_bwd = jnp.exp2(ref - bc)
        q = q_ref[0, rows, :]
        k = k_ref[0, rows, :]
        qd_ref[rows, :] = q * e_fwd.astype(BF16)
        qs_ref[rows, :] = q * (e_fwd * jnp.exp2(ref)).astype(BF16)
        kd_ref[rows, :] = k * e_bwd.astype(BF16)
        ks_ref[rows, :] = k * (e_bwd * jnp.exp2(bend - ref)).astype(BF16)
        eend_ref[c:c + 1, :] = jnp.exp2(bend)

    def phase2(c):
        rows = slice(c * CHUNK, (c + 1) * CHUNK)
        for h in range(GLA_H):
            ksl = slice(h * GLA_DK, (h + 1) * GLA_DK)
            vsl = slice(h * GLA_DV, (h + 1) * GLA_DV)
            v = v_ref[0, rows, vsl]
            sc = lax.dot_general(qd_ref[rows, ksl], kd_ref[rows, ksl], NT_DIMS,
                                 preferred_element_type=F32)
            sc = jnp.where(tri, sc, 0.0).astype(BF16)
            oi_ref[rows, vsl] = jnp.dot(sc, v, preferred_element_type=F32)
            kv_ref[c * GLA_H + h] = lax.dot_general(v, ks_ref[rows, ksl], TN_DIMS,
                                                    preferred_element_type=F32)

    def phase3(c):
        rows = slice(c * CHUNK, (c + 1) * CHUNK)
        for h in range(GLA_H):
            ksl = slice(h * GLA_DK, (h + 1) * GLA_DK)
            vsl = slice(h * GLA_DV, (h + 1) * GLA_DV)
            st = s_ref[h]
            o = oi_ref[rows, vsl] + lax.dot_general(qs_ref[rows, ksl], st.astype(BF16), NT_DIMS,
                                                    preferred_element_type=F32)
            s_ref[h] = st * eend_ref[c:c + 1, ksl] + kv_ref[c * GLA_H + h]
            o = _rms(o, gn_ref[...]) * r_ref[0, rows, vsl].astype(F32)
            o_ref[0, rows, vsl] = o.astype(BF16)

    _skewed(CHUNKS_PER_TILE, (phase1, phase2, phase3))


def _gla(big, la, gn, tiles_per_batch):
    r = big.shape[0]
    return pl.pallas_call(
        functools.partial(_gla_kernel, tiles_per_batch),
        grid=(r,),
        in_specs=[
            pl.BlockSpec((1, TM, GLA_QK), lambda i: (i, 0, OFF_GQ // GLA_QK)),
            pl.BlockSpec((1, TM, GLA_QK), lambda i: (i, 0, OFF_GK // GLA_QK)),
            pl.BlockSpec((1, TM, GLA_V), lambda i: (i, 0, OFF_GV // GLA_V)),
            pl.BlockSpec((1, TM, GLA_V), lambda i: (i, 0, OFF_GR // GLA_V)),
            pl.BlockSpec((1, TM, GLA_QK), lambda i: (i, 0, 0)),
            pl.BlockSpec((1, GLA_DV), lambda i: (0, 0)),
        ],
        out_specs=pl.BlockSpec((1, TM, GLA_V), lambda i: (i, 0, 0)),
        out_shape=jax.ShapeDtypeStruct((r, TM, GLA_V), BF16),
        scratch_shapes=[
            pltpu.VMEM((GLA_H, GLA_DV, GLA_DK), F32),
            pltpu.VMEM((TM, GLA_QK), BF16),
            pltpu.VMEM((TM, GLA_QK), BF16),
            pltpu.VMEM((TM, GLA_QK), BF16),
            pltpu.VMEM((TM, GLA_QK), BF16),
            pltpu.VMEM((16, GLA_QK), F32),
            pltpu.VMEM((TM, GLA_V), F32),
            pltpu.VMEM((CHUNKS_PER_TILE * GLA_H, GLA_DV, GLA_DK), F32),
        ],
        compiler_params=_params(("arbitrary",)),
        name="gla",
    )(big, big, big, big, la, gn)


def _mlstm_kernel(tiles_per_batch, q_ref, k_ref, v_ref, og_ref, mg_ref, mn_ref, o_ref,
                  c_ref, n_ref, m_ref):
    @pl.when(pl.program_id(0) % tiles_per_batch == 0)
    def _():
        c_ref[...] = jnp.zeros_like(c_ref)
        n_ref[...] = jnp.zeros_like(n_ref)
        m_ref[...] = jnp.zeros_like(m_ref)

    tri = _tri(CHUNK)
    tri_b = tri.astype(BF16)

    def chunk(c, carry):
        rows = pl.ds(pl.multiple_of(c * CHUNK, CHUNK), CHUNK)
        sm = mg_ref[0, rows, :]
        fc = _cumsum_rows(tri_b, sm)
        fct = fc.T
        smt = sm.T
        for h in range(ML_H):
            ksl = slice(h * ML_DK, (h + 1) * ML_DK)
            vsl = slice(h * ML_DV, (h + 1) * ML_DV)
            f_col = fc[:, SMALL_F + h:SMALL_F + h + 1]
            f_row = fct[SMALL_F + h:SMALL_F + h + 1, :]
            li_col = sm[:, SMALL_I + h:SMALL_I + h + 1]
            li_row = smt[SMALL_I + h:SMALL_I + h + 1, :]
            m_prev = m_ref[h, 0:1, 0:1]
            d_intra = jnp.where(tri, f_col - f_row + li_row, -jnp.inf)
            d_inter = f_col + m_prev
            m = jnp.maximum(d_inter, jnp.max(d_intra, axis=-1, keepdims=True))
            w_intra = jnp.exp2(d_intra - m)
            w_inter = jnp.exp2(d_inter - m)
            qh = q_ref[0, rows, ksl]
            kh = k_ref[0, rows, ksl]
            v = v_ref[0, rows, vsl]
            ct = c_ref[h]
            n_prev = n_ref[h, 0:1, :]
            qkw = lax.dot_general(qh, kh, NT_DIMS, preferred_element_type=F32) * w_intra
            num = w_inter * lax.dot_general(qh, ct.astype(BF16), NT_DIMS, preferred_element_type=F32)
            num = num + jnp.dot(qkw.astype(BF16), v, preferred_element_type=F32)
            den = w_inter * jnp.sum(qh.astype(F32) * n_prev, axis=-1, keepdims=True)
            den = den + jnp.sum(qkw, axis=-1, keepdims=True)
            hh = num * (1.0 / jnp.maximum(jnp.abs(den), jnp.exp2(-m)))
            f_last = f_col[CHUNK - 1:CHUNK, :]
            d_state = f_last - f_col + li_col
            m_new = jnp.maximum(f_last + m_prev, jnp.max(d_state, axis=0, keepdims=True))
            g_prev = jnp.exp2(f_last + m_prev - m_new)
            kw = kh.astype(F32) * jnp.exp2(d_state - m_new)
            c_ref[h] = g_prev * ct + lax.dot_general(v, kw.astype(BF16), TN_DIMS,
                                                     preferred_element_type=F32)
            n_ref[h] = jnp.broadcast_to(g_prev * n_prev + jnp.sum(kw, axis=0, keepdims=True),
                                        (8, ML_DK))
            m_ref[h] = jnp.broadcast_to(m_new, (8, 128))
            hh = _rms(hh, mn_ref[:, vsl]) * og_ref[0, rows, vsl].astype(F32)
            o_ref[0, rows, vsl] = hh.astype(BF16)
        return carry

    lax.fori_loop(0, CHUNKS_PER_TILE, chunk, 0)


def _mlstm(big, mg, mn, tiles_per_batch):
    r = big.shape[0]
    return pl.pallas_call(
        functools.partial(_mlstm_kernel, tiles_per_batch),
        grid=(r,),
        in_specs=[
            pl.BlockSpec((1, TM, ML_QK), lambda i: (i, 0, OFF_MQ // ML_QK)),
            pl.BlockSpec((1, TM, ML_QK), lambda i: (i, 0, OFF_MK // ML_QK)),
            pl.BlockSpec((1, TM, ML_V), lambda i: (i, 0, OFF_MV // ML_V)),
            pl.BlockSpec((1, TM, ML_V), lambda i: (i, 0, OFF_MO // ML_V)),
            pl.BlockSpec((1, TM, N_SMALL), lambda i: (i, 0, 0)),
            pl.BlockSpec((1, ML_V), lambda i: (0, 0)),
        ],
        out_specs=pl.BlockSpec((1, TM, ML_V), lambda i: (i, 0, 0)),
        out_shape=jax.ShapeDtypeStruct((r, TM, ML_V), BF16),
        scratch_shapes=[
            pltpu.VMEM((ML_H, ML_DV, ML_DK), F32),
            pltpu.VMEM((ML_H, 8, ML_DK), F32),
            pltpu.VMEM((ML_H, 8, 128), F32),
        ],
        compiler_params=_params(("arbitrary",)),
        name="mlstm",
    )(big, big, big, big, mg, mn)


def _merge_kernel(og_ref, om_ref, ga_ref, gb_ref, h_ref, wg_ref, wm_ref, wo_ref, out_ref):
    a = jnp.dot(og_ref[0], wg_ref[...], preferred_element_type=F32)
    b = jnp.dot(om_ref[0], wm_ref[...], preferred_element_type=F32)
    y = ga_ref[0].astype(F32) * a + gb_ref[0].astype(F32) * b
    out_ref[0] = h_ref[0] + jnp.dot(y.astype(BF16), wo_ref[...], preferred_element_type=F32)


def _merge(o_gla, o_ml, big, h, wg, wm, wo):
    r = h.shape[0]
    wspec = pl.BlockSpec((D_MODEL, D_MODEL), lambda i: (0, 0))
    return pl.pallas_call(
        _merge_kernel,
        grid=(r,),
        in_specs=[
            pl.BlockSpec((1, TM, GLA_V), lambda i: (i, 0, 0)),
            pl.BlockSpec((1, TM, ML_V), lambda i: (i, 0, 0)),
            pl.BlockSpec((1, TM, D_MODEL), lambda i: (i, 0, OFF_GA // D_MODEL)),
            pl.BlockSpec((1, TM, D_MODEL), lambda i: (i, 0, OFF_GB // D_MODEL)),
            pl.BlockSpec((1, TM, D_MODEL), lambda i: (i, 0, 0)),
            wspec, wspec, wspec,
        ],
        out_specs=pl.BlockSpec((1, TM, D_MODEL), lambda i: (i, 0, 0)),
        out_shape=jax.ShapeDtypeStruct(h.shape, F32),
        input_output_aliases={4: 0},
        compiler_params=_params(("arbitrary",)),
        name="merge",
    )(o_gla, o_ml, big, big, h, wg, wm, wo)


def _ffn_kernel(tiles_per_batch, h_ref, halo_ref, g_ref, wup_ref, cw_ref, cb_ref, wdn_ref,
                out_ref, xn_ref, a_ref):
    keep = (pl.program_id(0) % tiles_per_batch != 0).astype(F32)
    xn_ref[0:HALO, :] = (_rms(halo_ref[0], g_ref[...]) * keep).astype(BF16)
    xn_ref[HALO:HALO + TM, :] = _rms(h_ref[0], g_ref[...]).astype(BF16)

    def conv_up(cols):
        u = jnp.dot(xn_ref[...], wup_ref[:, cols], preferred_element_type=F32)
        y = cb_ref[:, cols]
        for tap in range(FFN_CONV):
            lo = HALO - (FFN_CONV - 1) + tap
            y = y + cw_ref[tap:tap + 1, cols] * u[lo:lo + TM, :]
        return y

    for j in range(N_FF_TILES):
        gate = conv_up(slice(j * FF_TILE, (j + 1) * FF_TILE))
        up = conv_up(slice(D_FF + j * FF_TILE, D_FF + (j + 1) * FF_TILE))
        a_ref[:, j * FF_TILE:(j + 1) * FF_TILE] = (gate * _sigmoid(gate) * up).astype(BF16)

    out_ref[0] = h_ref[0] + jnp.dot(a_ref[...], wdn_ref[...], preferred_element_type=F32)


def _ffn(h, g, wup, cw, cb, wdn, tiles_per_batch):
    r = h.shape[0]
    last_halo_block = TM // HALO - 1
    const = lambda i: (0, 0)
    resident = pl.Buffered(1)
    return pl.pallas_call(
        functools.partial(_ffn_kernel, tiles_per_batch),
        grid=(r,),
        in_specs=[
            pl.BlockSpec((1, TM, D_MODEL), lambda i: (i, 0, 0)),
            pl.BlockSpec((1, HALO, D_MODEL), lambda i: (jnp.maximum(i - 1, 0), last_halo_block, 0)),
            pl.BlockSpec((1, D_MODEL), const),
            pl.BlockSpec((D_MODEL, 2 * D_FF), const, pipeline_mode=resident),
            pl.BlockSpec((FFN_CONV, 2 * D_FF), const),
            pl.BlockSpec((1, 2 * D_FF), const),
            pl.BlockSpec((D_FF, D_MODEL), const, pipeline_mode=resident),
        ],
        out_specs=pl.BlockSpec((1, TM, D_MODEL), lambda i: (i, 0, 0)),
        out_shape=jax.ShapeDtypeStruct(h.shape, F32),
        scratch_shapes=[
            pltpu.VMEM((HALO + TM, D_MODEL), BF16),
            pltpu.VMEM((TM, D_FF), BF16),
        ],
        compiler_params=_params(("arbitrary",)),
        name="ffn",
    )(h, h, g, wup, cw, cb, wdn)


def _final_kernel(a_ref, nxt_ref, g_ref, o_ref):
    y = jnp.concatenate([a_ref[0, N_META:, :], nxt_ref[0]], axis=0)
    o_ref[0] = _rms(y, g_ref[...])


def _final(h, g, seq):
    b, lp, _ = h.shape
    nxt_per_tile = TM // N_META
    last_nxt = lp // N_META - 1
    return pl.pallas_call(
        _final_kernel,
        grid=(b, lp // TM),
        in_specs=[
            pl.BlockSpec((1, TM, D_MODEL), lambda i, j: (i, j, 0)),
            pl.BlockSpec((1, N_META, D_MODEL),
                         lambda i, j: (i, jnp.minimum((j + 1) * nxt_per_tile, last_nxt), 0)),
            pl.BlockSpec((1, D_MODEL), lambda i, j: (0, 0)),
        ],
        out_specs=pl.BlockSpec((1, TM, D_MODEL), lambda i, j: (i, j, 0)),
        out_shape=jax.ShapeDtypeStruct((b, seq, D_MODEL), F32),
        compiler_params=_params(("arbitrary", "arbitrary")),
        name="final_norm",
    )(h, h, g)


def kernel(x, meta, norm_mix, w_in, b_in, gla_w_gate, gla_b_gate, gla_norm, ml_conv_w, ml_conv_b,
           ml_norm, w_branch_gla, w_branch_ml, w_out, norm_ffn, ffn_w_up, ffn_conv_w, ffn_conv_b,
           ffn_w_down, norm_final):
    b, seq, _ = x.shape
    depth = w_in.shape[0]
    lp = -(-(N_META + seq) // TM) * TM
    tiles_per_batch = lp // TM
    assert lp % CHUNK == 0

    h = jnp.concatenate([
        jnp.broadcast_to(meta[None].astype(x.dtype), (b, N_META, D_MODEL)), x,
        jnp.zeros((b, lp - N_META - seq, D_MODEL), x.dtype)], axis=1)
    h = h.reshape(b * tiles_per_batch, TM, D_MODEL)

    c_gr = 2 * GLA_QK + GLA_V
    c_lr = c_gr + GLA_V
    c_mq = c_lr + GLA_LR
    c_mo = c_mq + 2 * ML_QK + ML_V
    c_mi = c_mo + ML_V
    c_ga = c_mi + 2 * ML_H

    def regroup(t):
        big = jnp.concatenate([t[..., :c_gr], t[..., c_mq:c_mo], t[..., c_gr:c_lr],
                               t[..., c_mo:c_mi], t[..., c_ga:]], axis=-1)
        small = jnp.concatenate([t[..., c_lr:c_mq], t[..., c_mi:c_ga]], axis=-1)
        pad = [(0, 0)] * (t.ndim - 1) + [(0, N_SMALL - small.shape[-1])]
        return big, jnp.pad(small, pad)

    w_big, w_small = regroup(w_in)
    w_big, w_small = w_big.astype(BF16), w_small.astype(BF16)
    b_big, b_small = regroup(b_in[:, None, :])
    wg_gate = jnp.pad(gla_w_gate, ((0, 0), (0, N_SMALL - GLA_LR), (0, 0))).astype(BF16)
    wup = ffn_w_up.astype(BF16)
    wdn = ffn_w_down.astype(BF16)
    wbg = w_branch_gla.astype(BF16)
    wbm = w_branch_ml.astype(BF16)
    wo = w_out.astype(BF16)

    for l in range(depth):
        big, la, mg = _inproj(h, norm_mix[l][None], w_big[l], b_big[l], w_small[l], b_small[l],
                              wg_gate[l], gla_b_gate[l][None], ml_conv_w[l], ml_conv_b[l][None],
                              tiles_per_batch)
        o_gla = _gla(big, la, gla_norm[l][None], tiles_per_batch)
        o_ml = _mlstm(big, mg, ml_norm[l][None], tiles_per_batch)
        h = _merge(o_gla, o_ml, big, h, wbg[l], wbm[l], wo[l])
        h = _ffn(h, norm_ffn[l][None], wup[l], ffn_conv_w[l], ffn_conv_b[l][None], wdn[l],
                 tiles_per_batch)

    return _final(h.reshape(b, lp, D_MODEL), norm_final[None], seq)
```

```python
import functools
import math

import jax
import jax.numpy as jnp
from jax import lax
from jax.experimental import pallas as pl
from jax.experimental.pallas import tpu as pltpu

F32 = jnp.float32
BF16 = jnp.bfloat16

D_MODEL = 1024
N_META = 16
CHUNK = 64
GLA_H = 4
GLA_DK = 128
GLA_DV = 256
GLA_LR = 16
GLA_TAU = 16.0
ML_H = 4
ML_DK = 128
ML_DV = 256
ML_CONV = 4
D_FF = 2816
FFN_CONV = 3
EPS = 1e-6
LOG2E = math.log2(math.e)

GLA_QK = GLA_H * GLA_DK
GLA_V = GLA_H * GLA_DV
ML_QK = ML_H * ML_DK
ML_V = ML_H * ML_DV

TM = 832
CHUNKS_PER_TILE = TM // CHUNK

INPROJ_TN = 2048
N_BIG = 4 * INPROJ_TN
OFF_GQ, OFF_GK, OFF_GV = 0, 512, 1024
OFF_MQ, OFF_MK, OFF_MV = 2048, 2560, 3072
OFF_GR, OFF_MO = 4096, 5120
OFF_GA, OFF_GB = 6144, 7168
N_SMALL = 128
SMALL_I = 16
SMALL_F = 20

FF_TILE = 256
N_FF_TILES = D_FF // FF_TILE
HALO = 16
VMEM_LIMIT = 56 * 1024 * 1024
ML_CHUNKS = (256, 256, 256, 64)
GLA_SAFE_LOG2_DECAY = -3.0

NT_DIMS = (((1,), (1,)), ((), ()))
TN_DIMS = (((0,), (0,)), ((), ()))


def _rms(x, g):
    return x * lax.rsqrt(jnp.mean(x * x, axis=-1, keepdims=True) + EPS) * g


def _sigmoid(x):
    return 1.0 / (1.0 + jnp.exp2(x * (-LOG2E)))


def _log2_sigmoid(x):
    x2 = x * LOG2E
    return jnp.minimum(x2, 0.0) - jnp.log2(1.0 + jnp.exp2(-jnp.abs(x2)))


def _tri(n):
    r = lax.broadcasted_iota(jnp.int32, (n, n), 0)
    c = lax.broadcasted_iota(jnp.int32, (n, n), 1)
    return r >= c


def _cumsum_rows(tri_b, x):
    hi = x.astype(BF16)
    lo = (x - hi.astype(F32)).astype(BF16)
    return (jnp.dot(tri_b, hi, preferred_element_type=F32)
            + jnp.dot(tri_b, lo, preferred_element_type=F32))


def _heads(x, d):
    return jnp.stack([x[:, h * d:(h + 1) * d] for h in range(GLA_H)], axis=0)


def _params(sem):
    return pltpu.CompilerParams(dimension_semantics=sem, vmem_limit_bytes=VMEM_LIMIT)


def _skewed(n, phases):
    for i in range(n + len(phases) - 1):
        for k, phase in enumerate(phases):
            if 0 <= i - k < n:
                phase(i - k)


def _inproj_kernel(tiles_per_batch, h_ref, g_ref, w_ref, b_ref, ws_ref, bs_ref, wg_ref, bg_ref,
                   cw_ref, cb_ref, big_ref, la_ref, mg_ref, lamin_ref, xn_ref, carry_ref):
    j = pl.program_id(1)

    def project():
        return jnp.dot(xn_ref[...], w_ref[...], preferred_element_type=F32) + b_ref[...]

    @pl.when(j == 0)
    def _():
        xn = _rms(h_ref[0], g_ref[...]).astype(BF16)
        xn_ref[...] = xn
        small = jnp.dot(xn, ws_ref[...], preferred_element_type=F32) + bs_ref[...]
        x = jnp.dot(small.astype(BF16), wg_ref[...], preferred_element_type=F32) + bg_ref[...]
        la = _log2_sigmoid(x) * (1.0 / GLA_TAU)
        la_ref[0] = la
        lmin = jnp.min(la.reshape(TM // 8, 8, GLA_QK), axis=0)
        lamin_ref[0] = functools.reduce(
            jnp.minimum, [lmin[:, i * 128:(i + 1) * 128] for i in range(GLA_QK // 128)])
        lane = lax.broadcasted_iota(jnp.int32, small.shape, 1)
        is_f = jnp.logical_and(lane >= SMALL_F, lane < SMALL_F + ML_H)
        mg_ref[0] = jnp.where(is_f, _log2_sigmoid(small), small * LOG2E)
        acc = project()
        big_ref[0, :, :GLA_QK] = (acc[:, :GLA_QK] * (GLA_DK ** -0.5)).astype(BF16)
        big_ref[0, :, GLA_QK:] = acc[:, GLA_QK:].astype(BF16)

    @pl.when(j == 1)
    def _():
        @pl.when(pl.program_id(0) % tiles_per_batch == 0)
        def _():
            carry_ref[...] = jnp.zeros_like(carry_ref)

        acc = project()
        pre = acc[:, :2 * ML_QK]
        xe = jnp.concatenate([carry_ref[...], pre], axis=0)
        carry_ref[...] = pre[TM - 8:, :]
        y = cb_ref[...]
        for tap in range(ML_CONV):
            lo = 8 - (ML_CONV - 1) + tap
            y = y + cw_ref[tap:tap + 1, :] * xe[lo:lo + TM, :]
        qk = y * _sigmoid(y)
        big_ref[0, :, :ML_QK] = (qk[:, :ML_QK] * (ML_DK ** -0.5)).astype(BF16)
        big_ref[0, :, ML_QK:2 * ML_QK] = qk[:, ML_QK:].astype(BF16)
        big_ref[0, :, 2 * ML_QK:] = acc[:, 2 * ML_QK:].astype(BF16)

    @pl.when(j == 2)
    def _():
        acc = project()
        r = acc[:, :GLA_V]
        big_ref[0, :, :GLA_V] = (r * _sigmoid(r)).astype(BF16)
        big_ref[0, :, GLA_V:] = _sigmoid(acc[:, GLA_V:]).astype(BF16)

    @pl.when(j == 3)
    def _():
        big_ref[0] = _sigmoid(project()).astype(BF16)


def _inproj(h, g, w, b, ws, bs, wg, bg, cw, cb, tiles_per_batch):
    r = h.shape[0]
    const = lambda i, j: (0, 0)
    return pl.pallas_call(
        functools.partial(_inproj_kernel, tiles_per_batch),
        grid=(r, N_BIG // INPROJ_TN),
        in_specs=[
            pl.BlockSpec((1, TM, D_MODEL), lambda i, j: (i, 0, 0)),
            pl.BlockSpec((1, D_MODEL), const),
            pl.BlockSpec((D_MODEL, INPROJ_TN), lambda i, j: (0, j)),
            pl.BlockSpec((1, INPROJ_TN), lambda i, j: (0, j)),
            pl.BlockSpec((D_MODEL, N_SMALL), const),
            pl.BlockSpec((1, N_SMALL), const),
            pl.BlockSpec((N_SMALL, GLA_QK), const),
            pl.BlockSpec((1, GLA_QK), const),
            pl.BlockSpec((ML_CONV, 2 * ML_QK), const),
            pl.BlockSpec((1, 2 * ML_QK), const),
        ],
        out_specs=[
            pl.BlockSpec((1, TM, INPROJ_TN), lambda i, j: (i, 0, j)),
            pl.BlockSpec((1, TM, GLA_QK), lambda i, j: (i, 0, 0)),
            pl.BlockSpec((1, TM, N_SMALL), lambda i, j: (i, 0, 0)),
            pl.BlockSpec((1, 8, 128), lambda i, j: (i, 0, 0)),
        ],
        out_shape=[
            jax.ShapeDtypeStruct((r, TM, N_BIG), BF16),
            jax.ShapeDtypeStruct((r, TM, GLA_QK), F32),
            jax.ShapeDtypeStruct((r, TM, N_SMALL), F32),
            jax.ShapeDtypeStruct((r, 8, 128), F32),
        ],
        scratch_shapes=[
            pltpu.VMEM((TM, D_MODEL), BF16),
            pltpu.VMEM((8, 2 * ML_QK), F32),
        ],
        compiler_params=_params(("arbitrary", "arbitrary")),
        name="inproj",
    )(h, g, w, b, ws, bs, wg, bg, cw, cb)


def _gla_kernel(tiles_per_batch, q_ref, k_ref, v_ref, r_ref, la_ref, lamin_ref, gn_ref,
                o_ref, s_ref, qs_ref, qd_ref, kd_ref, ks_ref, eend_ref, oi_ref, kv_ref, bc_ref, qf_ref):
    @pl.when(pl.program_id(0) % tiles_per_batch == 0)
    def _():
        s_ref[...] = jnp.zeros_like(s_ref)

    safe = jnp.min(lamin_ref[0]) >= GLA_SAFE_LOG2_DECAY

    @pl.when(safe)
    def _():
        _gla_factored(q_ref, k_ref, v_ref, r_ref, la_ref, gn_ref, o_ref, s_ref, qs_ref, qd_ref,
                      kd_ref, ks_ref, eend_ref, oi_ref, kv_ref)

    @pl.when(jnp.logical_not(safe))
    def _():
        _gla_direct(q_ref, k_ref, v_ref, r_ref, la_ref, gn_ref, o_ref, s_ref, oi_ref, bc_ref, qf_ref)


def _gla_direct(q_ref, k_ref, v_ref, r_ref, la_ref, gn_ref, o_ref, s_ref, oi_ref, bc_ref, qf_ref):
    tri_b = _tri(CHUNK).astype(BF16)
    s_iota = lax.broadcasted_iota(jnp.int32, (CHUNK, 1), 0)

    def chunk(c, carry):
        r0 = pl.multiple_of(c * CHUNK, CHUNK)
        rows = pl.ds(r0, CHUNK)
        bc = _cumsum_rows(tri_b, la_ref[0, rows, :])
        bend = bc[CHUNK - 1:CHUNK, :]
        q = q_ref[0, rows, :].astype(F32)
        k = k_ref[0, rows, :].astype(F32)
        bc_ref[...] = bc
        qf_ref[...] = q
        qs = (q * jnp.exp2(bc)).astype(BF16)
        ks = (k * jnp.exp2(bend - bc)).astype(BF16)
        e_end = jnp.exp2(bend)
        for h in range(GLA_H):
            ksl = slice(h * GLA_DK, (h + 1) * GLA_DK)
            vsl = slice(h * GLA_DV, (h + 1) * GLA_DV)
            kh = k[:, ksl]
            bch = bc[:, ksl]
            vh = v_ref[0, rows, vsl].astype(F32)

            def group(g, carry2):
                g8 = pl.multiple_of(g * 8, 8)
                bt8 = bc_ref[pl.ds(g8, 8), ksl]
                qt8 = qf_ref[pl.ds(g8, 8), ksl]
                out_rows = []
                for j in range(8):
                    e = jnp.exp2(jnp.minimum(bt8[j:j + 1, :] - bch, 0.0))
                    col = jnp.sum(e * kh * qt8[j:j + 1, :], axis=-1, keepdims=True)
                    col = jnp.where(s_iota <= g8 + j, col, 0.0)
                    out_rows.append(jnp.sum(col * vh, axis=0, keepdims=True))
                oi_ref[pl.ds(r0 + g8, 8), vsl] = jnp.concatenate(out_rows, axis=0)
                return carry2

            lax.fori_loop(0, CHUNK // 8, group, 0)
            st = s_ref[h]
            o = oi_ref[rows, vsl] + lax.dot_general(qs[:, ksl], st.astype(BF16), NT_DIMS,
                                                    preferred_element_type=F32)
            s_ref[h] = st * e_end[:, ksl] + lax.dot_general(
                v_ref[0, rows, vsl], ks[:, ksl], TN_DIMS, preferred_element_type=F32)
            o = _rms(o, gn_ref[...]) * r_ref[0, rows, vsl].astype(F32)
            o_ref[0, rows, vsl] = o.astype(BF16)
        return carry

    lax.fori_loop(0, CHUNKS_PER_TILE, chunk, 0)


def _gla_factored(q_ref, k_ref, v_ref, r_ref, la_ref, gn_ref, o_ref, s_ref, qs_ref, qd_ref,
                  kd_ref, ks_ref, eend_ref, oi_ref, kv_ref):
    tri = _tri(CHUNK)
    tri_b = tri.astype(BF16)
    half = CHUNK // 2

    def phase1(c):
        rows = slice(c * CHUNK, (c + 1) * CHUNK)
        bc = _cumsum_rows(tri_b, la_ref[0, rows, :])
        ref = bc[half - 1:half, :]
        bend = bc[CHUNK - 1:CHUNK, :]
        e_fwd = jnp.exp2(bc - ref)
        e_bwd = jnp.exp2(ref - bc)
        q = q_ref[0, rows, :]
        k = k_ref[0, rows, :]
        qd_ref[rows, :] = q * e_fwd.astype(BF16)
        qs_ref[rows, :] = q * (e_fwd * jnp.exp2(ref)).astype(BF16)
        kd_ref[rows, :] = k * e_bwd.astype(BF16)
        ks_ref[rows, :] = k * (e_bwd * jnp.exp2(bend - ref)).astype(BF16)
        eend_ref[c:c + 1, :] = jnp.exp2(bend)

    BQK = (((2,), (2,)), ((0,), (0,)))
    BPV = (((2,), (1,)), ((0,), (0,)))
    BTN = (((1,), (1,)), ((0,), (0,)))

    def phase2(c):
        rows = slice(c * CHUNK, (c + 1) * CHUNK)
        v = _heads(v_ref[0, rows, :], GLA_DV)
        sc = lax.dot_general(_heads(qd_ref[rows, :], GLA_DK), _heads(kd_ref[rows, :], GLA_DK), BQK,
                             preferred_element_type=F32)
        sc = jnp.where(tri[None], sc, 0.0).astype(BF16)
        oi = lax.dot_general(sc, v, BPV, preferred_element_type=F32)
        for h in range(GLA_H):
            oi_ref[rows, h * GLA_DV:(h + 1) * GLA_DV] = oi[h]
        kv_ref[c] = lax.dot_general(v, _heads(ks_ref[rows, :], GLA_DK), BTN, preferred_element_type=F32)

    def phase3(c):
        rows = slice(c * CHUNK, (c + 1) * CHUNK)
        st = s_ref[...]
        o = _heads(oi_ref[rows, :], GLA_DV) + lax.dot_general(
            _heads(qs_ref[rows, :], GLA_DK), st.astype(BF16), BQK, preferred_element_type=F32)
        s_ref[...] = st * _heads(eend_ref[c:c + 1, :], GLA_DK) + kv_ref[c]
        o = o * lax.rsqrt(jnp.mean(o * o, axis=-1, keepdims=True) + EPS)
        for h in range(GLA_H):
            vsl = slice(h * GLA_DV, (h + 1) * GLA_DV)
            o_ref[0, rows, vsl] = (o[h] * gn_ref[...] * r_ref[0, rows, vsl].astype(F32)).astype(BF16)

    _skewed(CHUNKS_PER_TILE, (phase1, phase2, phase3))


def _gla(big, la, lamin, gn, tiles_per_batch):
    r = big.shape[0]
    return pl.pallas_call(
        functools.partial(_gla_kernel, tiles_per_batch),
        grid=(r,),
        in_specs=[
            pl.BlockSpec((1, TM, GLA_QK), lambda i: (i, 0, OFF_GQ // GLA_QK)),
            pl.BlockSpec((1, TM, GLA_QK), lambda i: (i, 0, OFF_GK // GLA_QK)),
            pl.BlockSpec((1, TM, GLA_V), lambda i: (i, 0, OFF_GV // GLA_V)),
            pl.BlockSpec((1, TM, GLA_V), lambda i: (i, 0, OFF_GR // GLA_V)),
            pl.BlockSpec((1, TM, GLA_QK), lambda i: (i, 0, 0)),
            pl.BlockSpec((1, 8, 128), lambda i: (i, 0, 0)),
            pl.BlockSpec((1, GLA_DV), lambda i: (0, 0)),
        ],
        out_specs=pl.BlockSpec((1, TM, GLA_V), lambda i: (i, 0, 0)),
        out_shape=jax.ShapeDtypeStruct((r, TM, GLA_V), BF16),
        scratch_shapes=[
            pltpu.VMEM((GLA_H, GLA_DV, GLA_DK), F32),
            pltpu.VMEM((TM, GLA_QK), BF16),
            pltpu.VMEM((TM, GLA_QK), BF16),
            pltpu.VMEM((TM, GLA_QK), BF16),
            pltpu.VMEM((TM, GLA_QK), BF16),
            pltpu.VMEM((16, GLA_QK), F32),
            pltpu.VMEM((TM, GLA_V), F32),
            pltpu.VMEM((CHUNKS_PER_TILE, GLA_H, GLA_DV, GLA_DK), F32),
            pltpu.VMEM((CHUNK, GLA_QK), F32),
            pltpu.VMEM((CHUNK, GLA_QK), F32),
        ],
        compiler_params=_params(("arbitrary",)),
        name="gla",
    )(big, big, big, big, la, lamin, gn)


def _mlstm_kernel(tiles_per_batch, q_ref, k_ref, v_ref, og_ref, mg_ref, mn_ref, o_ref,
                  c_ref, n_ref, m_ref):
    @pl.when(pl.program_id(0) % tiles_per_batch == 0)
    def _():
        c_ref[...] = jnp.zeros_like(c_ref)
        n_ref[...] = jnp.zeros_like(n_ref)
        m_ref[...] = jnp.zeros_like(m_ref)

    BQK = (((2,), (2,)), ((0,), (0,)))
    BPV = (((2,), (1,)), ((0,), (0,)))
    BTN = (((1,), (1,)), ((0,), (0,)))

    def chunk(r0, cs):
        tri = _tri(cs)
        tri_b = tri.astype(BF16)
        rows = slice(r0, r0 + cs)
        sm = mg_ref[0, rows, :]
        fc = _cumsum_rows(tri_b, sm)
        fct = fc.T
        smt = sm.T
        f_col = jnp.stack([fc[:, SMALL_F + h:SMALL_F + h + 1] for h in range(ML_H)], axis=0)
        li_col = jnp.stack([sm[:, SMALL_I + h:SMALL_I + h + 1] for h in range(ML_H)], axis=0)
        f_row = jnp.stack([fct[SMALL_F + h:SMALL_F + h + 1, :] for h in range(ML_H)], axis=0)
        li_row = jnp.stack([smt[SMALL_I + h:SMALL_I + h + 1, :] for h in range(ML_H)], axis=0)
        m_prev = m_ref[:, 0:1, 0:1]
        d_intra = jnp.where(tri[None], f_col - f_row + li_row, -jnp.inf)
        d_inter = f_col + m_prev
        m = jnp.maximum(d_inter, jnp.max(d_intra, axis=-1, keepdims=True))
        w_intra = jnp.exp2(d_intra - m)
        w_inter = jnp.exp2(d_inter - m)
        qh = _heads(q_ref[0, rows, :], ML_DK)
        kh = _heads(k_ref[0, rows, :], ML_DK)
        v = _heads(v_ref[0, rows, :], ML_DV)
        ct = c_ref[...]
        n_prev = n_ref[:, 0:1, :]
        qkw = lax.dot_general(qh, kh, BQK, preferred_element_type=F32) * w_intra
        num = w_inter * lax.dot_general(qh, ct.astype(BF16), BQK, preferred_element_type=F32)
        num = num + lax.dot_general(qkw.astype(BF16), v, BPV, preferred_element_type=F32)
        den = w_inter * jnp.sum(qh.astype(F32) * n_prev, axis=-1, keepdims=True)
        den = den + jnp.sum(qkw, axis=-1, keepdims=True)
        hh = num * (1.0 / jnp.maximum(jnp.abs(den), jnp.exp2(-m)))
        f_last = f_col[:, cs - 1:cs, :]
        d_state = f_last - f_col + li_col
        m_new = jnp.maximum(f_last + m_prev, jnp.max(d_state, axis=1, keepdims=True))
        g_prev = jnp.exp2(f_last + m_prev - m_new)
        kw = kh.astype(F32) * jnp.exp2(d_state - m_new)
        c_ref[...] = g_prev * ct + lax.dot_general(v, kw.astype(BF16), BTN, preferred_element_type=F32)
        n_ref[...] = jnp.broadcast_to(g_prev * n_prev + jnp.sum(kw, axis=1, keepdims=True),
                                      (ML_H, 8, ML_DK))
        m_ref[...] = jnp.broadcast_to(m_new, (ML_H, 8, 128))
        hh = hh * lax.rsqrt(jnp.mean(hh * hh, axis=-1, keepdims=True) + EPS)
        for h in range(ML_H):
            vsl = slice(h * ML_DV, (h + 1) * ML_DV)
            o_ref[0, rows, vsl] = (hh[h] * mn_ref[:, vsl] * og_ref[0, rows, vsl].astype(F32)).astype(BF16)

    r0 = 0
    for cs in ML_CHUNKS:
        chunk(r0, cs)
        r0 += cs


def _mlstm(big, mg, mn, tiles_per_batch):
    r = big.shape[0]
    return pl.pallas_call(
        functools.partial(_mlstm_kernel, tiles_per_batch),
        grid=(r,),
        in_specs=[
            pl.BlockSpec((1, TM, ML_QK), lambda i: (i, 0, OFF_MQ // ML_QK)),
            pl.BlockSpec((1, TM, ML_QK), lambda i: (i, 0, OFF_MK // ML_QK)),
            pl.BlockSpec((1, TM, ML_V), lambda i: (i, 0, OFF_MV // ML_V)),
            pl.BlockSpec((1, TM, ML_V), lambda i: (i, 0, OFF_MO // ML_V)),
            pl.BlockSpec((1, TM, N_SMALL), lambda i: (i, 0, 0)),
            pl.BlockSpec((1, ML_V), lambda i: (0, 0)),
        ],
        out_specs=pl.BlockSpec((1, TM, ML_V), lambda i: (i, 0, 0)),
        out_shape=jax.ShapeDtypeStruct((r, TM, ML_V), BF16),
        scratch_shapes=[
            pltpu.VMEM((ML_H, ML_DV, ML_DK), F32),
            pltpu.VMEM((ML_H, 8, ML_DK), F32),
            pltpu.VMEM((ML_H, 8, 128), F32),
        ],
        compiler_params=_params(("arbitrary",)),
        name="mlstm",
    )(big, big, big, big, mg, mn)


def _merge_kernel(og_ref, om_ref, ga_ref, gb_ref, h_ref, wg_ref, wm_ref, wo_ref, out_ref):
    a = jnp.dot(og_ref[0], wg_ref[...], preferred_element_type=F32)
    b = jnp.dot(om_ref[0], wm_ref[...], preferred_element_type=F32)
    y = ga_ref[0].astype(F32) * a + gb_ref[0].astype(F32) * b
    out_ref[0] = h_ref[0] + jnp.dot(y.astype(BF16), wo_ref[...], preferred_element_type=F32)


def _merge(o_gla, o_ml, big, h, wg, wm, wo):
    r = h.shape[0]
    wspec = pl.BlockSpec((D_MODEL, D_MODEL), lambda i: (0, 0))
    return pl.pallas_call(
        _merge_kernel,
        grid=(r,),
        in_specs=[
            pl.BlockSpec((1, TM, GLA_V), lambda i: (i, 0, 0)),
            pl.BlockSpec((1, TM, ML_V), lambda i: (i, 0, 0)),
            pl.BlockSpec((1, TM, D_MODEL), lambda i: (i, 0, OFF_GA // D_MODEL)),
            pl.BlockSpec((1, TM, D_MODEL), lambda i: (i, 0, OFF_GB // D_MODEL)),
            pl.BlockSpec((1, TM, D_MODEL), lambda i: (i, 0, 0)),
            wspec, wspec, wspec,
        ],
        out_specs=pl.BlockSpec((1, TM, D_MODEL), lambda i: (i, 0, 0)),
        out_shape=jax.ShapeDtypeStruct(h.shape, F32),
        input_output_aliases={4: 0},
        compiler_params=_params(("arbitrary",)),
        name="merge",
    )(o_gla, o_ml, big, big, h, wg, wm, wo)


def _ffn_kernel(tiles_per_batch, h_ref, halo_ref, g_ref, wup_ref, cw_ref, cb_ref, wdn_ref,
                out_ref, xn_ref, a_ref):
    keep = (pl.program_id(0) % tiles_per_batch != 0).astype(F32)
    xn_ref[0:HALO, :] = (_rms(halo_ref[0], g_ref[...]) * keep).astype(BF16)
    xn_ref[HALO:HALO + TM, :] = _rms(h_ref[0], g_ref[...]).astype(BF16)

    def conv_up(cols):
        u = jnp.dot(xn_ref[...], wup_ref[:, cols], preferred_element_type=F32)
        y = cb_ref[:, cols]
        for tap in range(FFN_CONV):
            lo = HALO - (FFN_CONV - 1) + tap
            y = y + cw_ref[tap:tap + 1, cols] * u[lo:lo + TM, :]
        return y

    for j in range(N_FF_TILES):
        gate = conv_up(slice(j * FF_TILE, (j + 1) * FF_TILE))
        up = conv_up(slice(D_FF + j * FF_TILE, D_FF + (j + 1) * FF_TILE))
        a_ref[:, j * FF_TILE:(j + 1) * FF_TILE] = (gate * _sigmoid(gate) * up).astype(BF16)

    out_ref[0] = h_ref[0] + jnp.dot(a_ref[...], wdn_ref[...], preferred_element_type=F32)


def _ffn(h, g, wup, cw, cb, wdn, tiles_per_batch):
    r = h.shape[0]
    last_halo_block = TM // HALO - 1
    const = lambda i: (0, 0)
    resident = pl.Buffered(1)
    return pl.pallas_call(
        functools.partial(_ffn_kernel, tiles_per_batch),
        grid=(r,),
        in_specs=[
            pl.BlockSpec((1, TM, D_MODEL), lambda i: (i, 0, 0)),
            pl.BlockSpec((1, HALO, D_MODEL), lambda i: (jnp.maximum(i - 1, 0), last_halo_block, 0)),
            pl.BlockSpec((1, D_MODEL), const),
            pl.BlockSpec((D_MODEL, 2 * D_FF), const, pipeline_mode=resident),
            pl.BlockSpec((FFN_CONV, 2 * D_FF), const),
            pl.BlockSpec((1, 2 * D_FF), const),
            pl.BlockSpec((D_FF, D_MODEL), const, pipeline_mode=resident),
        ],
        out_specs=pl.BlockSpec((1, TM, D_MODEL), lambda i: (i, 0, 0)),
        out_shape=jax.ShapeDtypeStruct(h.shape, F32),
        scratch_shapes=[
            pltpu.VMEM((HALO + TM, D_MODEL), BF16),
            pltpu.VMEM((TM, D_FF), BF16),
        ],
        compiler_params=_params(("arbitrary",)),
        name="ffn",
    )(h, h, g, wup, cw, cb, wdn)


def _final_kernel(a_ref, nxt_ref, g_ref, o_ref):
    y = jnp.concatenate([a_ref[0, N_META:, :], nxt_ref[0]], axis=0)
    o_ref[0] = _rms(y, g_ref[...])


def _final(h, g, seq):
    b, lp, _ = h.shape
    nxt_per_tile = TM // N_META
    last_nxt = lp // N_META - 1
    return pl.pallas_call(
        _final_kernel,
        grid=(b, lp // TM),
        in_specs=[
            pl.BlockSpec((1, TM, D_MODEL), lambda i, j: (i, j, 0)),
            pl.BlockSpec((1, N_META, D_MODEL),
                         lambda i, j: (i, jnp.minimum((j + 1) * nxt_per_tile, last_nxt), 0)),
            pl.BlockSpec((1, D_MODEL), lambda i, j: (0, 0)),
        ],
        out_specs=pl.BlockSpec((1, TM, D_MODEL), lambda i, j: (i, j, 0)),
        out_shape=jax.ShapeDtypeStruct((b, seq, D_MODEL), F32),
        compiler_params=_params(("arbitrary", "arbitrary")),
        name="final_norm",
    )(h, h, g)


def kernel(x, meta, norm_mix, w_in, b_in, gla_w_gate, gla_b_gate, gla_norm, ml_conv_w, ml_conv_b,
           ml_norm, w_branch_gla, w_branch_ml, w_out, norm_ffn, ffn_w_up, ffn_conv_w, ffn_conv_b,
           ffn_w_down, norm_final):
    b, seq, _ = x.shape
    depth = w_in.shape[0]
    lp = -(-(N_META + seq) // TM) * TM
    tiles_per_batch = lp // TM
    assert lp % CHUNK == 0

    h = jnp.concatenate([
        jnp.broadcast_to(meta[None].astype(x.dtype), (b, N_META, D_MODEL)), x,
        jnp.zeros((b, lp - N_META - seq, D_MODEL), x.dtype)], axis=1)
    h = h.reshape(b * tiles_per_batch, TM, D_MODEL)

    c_gr = 2 * GLA_QK + GLA_V
    c_lr = c_gr + GLA_V
    c_mq = c_lr + GLA_LR
    c_mo = c_mq + 2 * ML_QK + ML_V
    c_mi = c_mo + ML_V
    c_ga = c_mi + 2 * ML_H

    def regroup(t):
        big = jnp.concatenate([t[..., :c_gr], t[..., c_mq:c_mo], t[..., c_gr:c_lr],
                               t[..., c_mo:c_mi], t[..., c_ga:]], axis=-1)
        small = jnp.concatenate([t[..., c_lr:c_mq], t[..., c_mi:c_ga]], axis=-1)
        pad = [(0, 0)] * (t.ndim - 1) + [(0, N_SMALL - small.shape[-1])]
        return big, jnp.pad(small, pad)

    w_big, w_small = regroup(w_in)
    w_big, w_small = w_big.astype(BF16), w_small.astype(BF16)
    b_big, b_small = regroup(b_in[:, None, :])
    wg_gate = jnp.pad(gla_w_gate, ((0, 0), (0, N_SMALL - GLA_LR), (0, 0))).astype(BF16)
    wup = ffn_w_up.astype(BF16)
    wdn = ffn_w_down.astype(BF16)
    wbg = w_branch_gla.astype(BF16)
    wbm = w_branch_ml.astype(BF16)
    wo = w_out.astype(BF16)

    for l in range(depth):
        big, la, mg, lamin = _inproj(h, norm_mix[l][None], w_big[l], b_big[l], w_small[l], b_small[l],
                                     wg_gate[l], gla_b_gate[l][None], ml_conv_w[l],
                                     ml_conv_b[l][None], tiles_per_batch)
        o_gla = _gla(big, la, lamin, gla_norm[l][None], tiles_per_batch)
        o_ml = _mlstm(big, mg, ml_norm[l][None], tiles_per_batch)
        h = _merge(o_gla, o_ml, big, h, wbg[l], wbm[l], wo[l])
        h = _ffn(h, norm_ffn[l][None], wup[l], ffn_conv_w[l], ffn_conv_b[l][None], wdn[l],
                 tiles_per_batch)

    return _final(h.reshape(b, lp, D_MODEL), norm_final[None], seq)
```

```python
import functools
import math

import jax
import jax.numpy as jnp
from jax import lax
from jax.experimental import pallas as pl
from jax.experimental.pallas import tpu as pltpu

F32 = jnp.float32
BF16 = jnp.bfloat16

D_MODEL = 1024
N_META = 16
CHUNK = 64
BLOCK = 64
GLA_H = 4
GLA_DK = 128
GLA_DV = 256
GLA_LR = 16
GLA_TAU = 16.0
ML_H = 4
ML_DK = 128
ML_DV = 256
ML_CONV = 4
D_FF = 2816
FFN_CONV = 3
EPS = 1e-6
LOG2E = math.log2(math.e)

GLA_QK = GLA_H * GLA_DK
GLA_V = GLA_H * GLA_DV
ML_QK = ML_H * ML_DK
ML_V = ML_H * ML_DV

TM = 832
CHUNKS_PER_TILE = TM // CHUNK

INPROJ_TN = 2048
N_BIG = 4 * INPROJ_TN
OFF_GQ, OFF_GK, OFF_GV = 0, 512, 1024
OFF_MQ, OFF_MK, OFF_MV = 2048, 2560, 3072
OFF_GR, OFF_MO = 4096, 5120
OFF_GA, OFF_GB = 6144, 7168
N_SMALL = 128
SMALL_I = 16
SMALL_F = 20

FF_TILE = 256
N_FF_TILES = D_FF // FF_TILE
HALO = 16
VMEM_LIMIT = 56 * 1024 * 1024
ML_CHUNKS = (320, 256, 256)
CONV_CARRY = 8 * (ML_CONV - 1)
GLA_SAFE_LOG2_DECAY = -3.0

NT_DIMS = (((1,), (1,)), ((), ()))
TN_DIMS = (((0,), (0,)), ((), ()))


def _rms(x, g):
    return x * lax.rsqrt(jnp.mean(x * x, axis=-1, keepdims=True) + EPS) * g


def _sigmoid(x):
    return 1.0 / (1.0 + jnp.exp2(x * (-LOG2E)))


def _log2_sigmoid(x):
    x2 = x * LOG2E
    return jnp.minimum(x2, 0.0) - jnp.log2(1.0 + jnp.exp2(-jnp.abs(x2)))


def _row_time(r):
    blk = jnp.bitwise_and(r, -BLOCK)
    return blk + jnp.left_shift(jnp.bitwise_and(r, 7), 3) + jnp.bitwise_and(jnp.right_shift(r, 3), 7)


def _row_of_time(t):
    return (t // BLOCK) * BLOCK + (t % 8) * 8 + (t // 8) % 8


def _causal(n):
    t_r = _row_time(lax.broadcasted_iota(jnp.int32, (n, n), 0))
    t_c = _row_time(lax.broadcasted_iota(jnp.int32, (n, n), 1))
    return t_r >= t_c


def _block_swap(x):
    n, c = x.shape
    return jnp.swapaxes(x.reshape(n // BLOCK, 8, 8, c), 1, 2).reshape(n, c)


def _time_shift(x, k, prev_tail):
    n, c = x.shape
    x4 = x.reshape(n // BLOCK, 8, 8, c)
    tail = x4[:, 8 - k:]
    prev = jnp.concatenate([prev_tail.reshape(1, k, 8, c), tail[:-1]], axis=0)
    sub = lax.broadcasted_iota(jnp.int32, tail.shape, 2)
    wrapped = jnp.where(sub == 0, pltpu.roll(prev, 1, axis=2), pltpu.roll(tail, 1, axis=2))
    return jnp.concatenate([wrapped, x4[:, :8 - k]], axis=1).reshape(n, c)


def _cumsum_rows(tri_b, x):
    hi = x.astype(BF16)
    lo = (x - hi.astype(F32)).astype(BF16)
    return (jnp.dot(tri_b, hi, preferred_element_type=F32)
            + jnp.dot(tri_b, lo, preferred_element_type=F32))


def _heads(x, d):
    return jnp.stack([x[:, h * d:(h + 1) * d] for h in range(GLA_H)], axis=0)


def _params(sem):
    return pltpu.CompilerParams(dimension_semantics=sem, vmem_limit_bytes=VMEM_LIMIT)


def _skewed(n, phases):
    for i in range(n + len(phases) - 1):
        for k, phase in enumerate(phases):
            if 0 <= i - k < n:
                phase(i - k)


def _inproj_kernel(tiles_per_batch, h_ref, g_ref, w_ref, b_ref, ws_ref, bs_ref, wg_ref, bg_ref,
                   cw_ref, cb_ref, big_ref, la_ref, mg_ref, lamin_ref, xn_ref, carry_ref):
    j = pl.program_id(1)

    def project():
        return jnp.dot(xn_ref[...], w_ref[...], preferred_element_type=F32) + b_ref[...]

    @pl.when(j == 0)
    def _():
        xn = _rms(h_ref[0], g_ref[...]).astype(BF16)
        xn_ref[...] = xn
        small = jnp.dot(xn, ws_ref[...], preferred_element_type=F32) + bs_ref[...]
        x = jnp.dot(small.astype(BF16), wg_ref[...], preferred_element_type=F32) + bg_ref[...]
        la = _log2_sigmoid(x) * (1.0 / GLA_TAU)
        la_ref[0] = la
        lmin = jnp.min(la.reshape(TM // 8, 8, GLA_QK), axis=0)
        lamin_ref[0] = functools.reduce(
            jnp.minimum, [lmin[:, i * 128:(i + 1) * 128] for i in range(GLA_QK // 128)])
        lane = lax.broadcasted_iota(jnp.int32, small.shape, 1)
        is_f = jnp.logical_and(lane >= SMALL_F, lane < SMALL_F + ML_H)
        mg_ref[0] = jnp.where(is_f, _log2_sigmoid(small), small * LOG2E)
        acc = project()
        big_ref[0, :, :GLA_QK] = (acc[:, :GLA_QK] * (GLA_DK ** -0.5)).astype(BF16)
        big_ref[0, :, GLA_QK:] = acc[:, GLA_QK:].astype(BF16)

    @pl.when(j == 1)
    def _():
        @pl.when(pl.program_id(0) % tiles_per_batch == 0)
        def _():
            carry_ref[...] = jnp.zeros_like(carry_ref)

        acc = project()
        pre = acc[:, :2 * ML_QK]
        y = cb_ref[...] + cw_ref[ML_CONV - 1:ML_CONV, :] * pre
        for tap in range(ML_CONV - 1):
            k = ML_CONV - 1 - tap
            y = y + cw_ref[tap:tap + 1, :] * _time_shift(pre, k, carry_ref[CONV_CARRY - 8 * k:, :])
        carry_ref[...] = pre[TM - CONV_CARRY:, :]
        qk = y * _sigmoid(y)
        big_ref[0, :, :ML_QK] = (qk[:, :ML_QK] * (ML_DK ** -0.5)).astype(BF16)
        big_ref[0, :, ML_QK:2 * ML_QK] = qk[:, ML_QK:].astype(BF16)
        big_ref[0, :, 2 * ML_QK:] = acc[:, 2 * ML_QK:].astype(BF16)

    @pl.when(j == 2)
    def _():
        acc = project()
        r = acc[:, :GLA_V]
        big_ref[0, :, :GLA_V] = (r * _sigmoid(r)).astype(BF16)
        big_ref[0, :, GLA_V:] = _sigmoid(acc[:, GLA_V:]).astype(BF16)

    @pl.when(j == 3)
    def _():
        big_ref[0] = _sigmoid(project()).astype(BF16)


def _inproj(h, g, w, b, ws, bs, wg, bg, cw, cb, tiles_per_batch):
    r = h.shape[0]
    const = lambda i, j: (0, 0)
    return pl.pallas_call(
        functools.partial(_inproj_kernel, tiles_per_batch),
        grid=(r, N_BIG // INPROJ_TN),
        in_specs=[
            pl.BlockSpec((1, TM, D_MODEL), lambda i, j: (i, 0, 0)),
            pl.BlockSpec((1, D_MODEL), const),
            pl.BlockSpec((D_MODEL, INPROJ_TN), lambda i, j: (0, j)),
            pl.BlockSpec((1, INPROJ_TN), lambda i, j: (0, j)),
            pl.BlockSpec((D_MODEL, N_SMALL), const),
            pl.BlockSpec((1, N_SMALL), const),
            pl.BlockSpec((N_SMALL, GLA_QK), const),
            pl.BlockSpec((1, GLA_QK), const),
            pl.BlockSpec((ML_CONV, 2 * ML_QK), const),
            pl.BlockSpec((1, 2 * ML_QK), const),
        ],
        out_specs=[
            pl.BlockSpec((1, TM, INPROJ_TN), lambda i, j: (i, 0, j)),
            pl.BlockSpec((1, TM, GLA_QK), lambda i, j: (i, 0, 0)),
            pl.BlockSpec((1, TM, N_SMALL), lambda i, j: (i, 0, 0)),
            pl.BlockSpec((1, 8, 128), lambda i, j: (i, 0, 0)),
        ],
        out_shape=[
            jax.ShapeDtypeStruct((r, TM, N_BIG), BF16),
            jax.ShapeDtypeStruct((r, TM, GLA_QK), F32),
            jax.ShapeDtypeStruct((r, TM, N_SMALL), F32),
            jax.ShapeDtypeStruct((r, 8, 128), F32),
        ],
        scratch_shapes=[
            pltpu.VMEM((TM, D_MODEL), BF16),
            pltpu.VMEM((CONV_CARRY, 2 * ML_QK), F32),
        ],
        compiler_params=_params(("arbitrary", "arbitrary")),
        name="inproj",
    )(h, g, w, b, ws, bs, wg, bg, cw, cb)


def _gla_kernel(tiles_per_batch, q_ref, k_ref, v_ref, r_ref, la_ref, lamin_ref, gn_ref,
                o_ref, s_ref, qs_ref, qd_ref, kd_ref, ks_ref, eend_ref, oi_ref, kv_ref, bc_ref, qf_ref):
    @pl.when(pl.program_id(0) % tiles_per_batch == 0)
    def _():
        s_ref[...] = jnp.zeros_like(s_ref)

    safe = jnp.min(lamin_ref[0]) >= GLA_SAFE_LOG2_DECAY

    @pl.when(safe)
    def _():
        _gla_factored(q_ref, k_ref, v_ref, r_ref, la_ref, gn_ref, o_ref, s_ref, qs_ref, qd_ref,
                      kd_ref, ks_ref, eend_ref, oi_ref, kv_ref)

    @pl.when(jnp.logical_not(safe))
    def _():
        _gla_direct(q_ref, k_ref, v_ref, r_ref, la_ref, gn_ref, o_ref, s_ref, oi_ref, bc_ref, qf_ref)


def _gla_direct(q_ref, k_ref, v_ref, r_ref, la_ref, gn_ref, o_ref, s_ref, oi_ref, bc_ref, qf_ref):
    tri_b = _causal(CHUNK).astype(BF16)
    s_time = _row_time(lax.broadcasted_iota(jnp.int32, (CHUNK, 1), 0))

    def chunk(c, carry):
        r0 = pl.multiple_of(c * CHUNK, CHUNK)
        rows = pl.ds(r0, CHUNK)
        bc = _cumsum_rows(tri_b, la_ref[0, rows, :])
        bend = bc[CHUNK - 1:CHUNK, :]
        q = q_ref[0, rows, :].astype(F32)
        k = k_ref[0, rows, :].astype(F32)
        bc_ref[...] = bc
        qf_ref[...] = q
        qs = (q * jnp.exp2(bc)).astype(BF16)
        ks = (k * jnp.exp2(bend - bc)).astype(BF16)
        e_end = jnp.exp2(bend)
        for h in range(GLA_H):
            ksl = slice(h * GLA_DK, (h + 1) * GLA_DK)
            vsl = slice(h * GLA_DV, (h + 1) * GLA_DV)
            kh = k[:, ksl]
            bch = bc[:, ksl]
            vh = v_ref[0, rows, vsl].astype(F32)

            def group(g, carry2):
                g8 = pl.multiple_of(g * 8, 8)
                bt8 = bc_ref[pl.ds(g8, 8), ksl]
                qt8 = qf_ref[pl.ds(g8, 8), ksl]
                out_rows = []
                for j in range(8):
                    e = jnp.exp2(jnp.minimum(bt8[j:j + 1, :] - bch, 0.0))
                    col = jnp.sum(e * kh * qt8[j:j + 1, :], axis=-1, keepdims=True)
                    col = jnp.where(s_time <= 8 * j + g, col, 0.0)
                    out_rows.append(jnp.sum(col * vh, axis=0, keepdims=True))
                oi_ref[pl.ds(r0 + g8, 8), vsl] = jnp.concatenate(out_rows, axis=0)
                return carry2

            lax.fori_loop(0, CHUNK // 8, group, 0)
            st = s_ref[h]
            o = oi_ref[rows, vsl] + lax.dot_general(qs[:, ksl], st.astype(BF16), NT_DIMS,
                                                    preferred_element_type=F32)
            s_ref[h] = st * e_end[:, ksl] + lax.dot_general(
                v_ref[0, rows, vsl], ks[:, ksl], TN_DIMS, preferred_element_type=F32)
            o = _rms(o, gn_ref[...]) * r_ref[0, rows, vsl].astype(F32)
            o_ref[0, rows, vsl] = o.astype(BF16)
        return carry

    lax.fori_loop(0, CHUNKS_PER_TILE, chunk, 0)


def _gla_factored(q_ref, k_ref, v_ref, r_ref, la_ref, gn_ref, o_ref, s_ref, qs_ref, qd_ref,
                  kd_ref, ks_ref, eend_ref, oi_ref, kv_ref):
    tri = _causal(CHUNK)
    tri_b = tri.astype(BF16)
    mid_row = _row_of_time(CHUNK // 2 - 1)

    def phase1(c):
        rows = slice(c * CHUNK, (c + 1) * CHUNK)
        bc = _cumsum_rows(tri_b, la_ref[0, rows, :])
        ref = bc[mid_row:mid_row + 1, :]
        bend = bc[CHUNK - 1:CHUNK, :]
        e_fwd = jnp.exp2(bc - ref)
        e_bwd = jnp.exp2(ref - bc)
        q = q_ref[0, rows, :]
        k = k_ref[0, rows, :]
        qd_ref[rows, :] = q * e_fwd.astype(BF16)
        qs_ref[rows, :] = q * (e_fwd * jnp.exp2(ref)).astype(BF16)
        kd_ref[rows, :] = k * e_bwd.astype(BF16)
        ks_ref[rows, :] = k * (e_bwd * jnp.exp2(bend - ref)).astype(BF16)
        eend_ref[c:c + 1, :] = jnp.exp2(bend)

    BQK = (((2,), (2,)), ((0,), (0,)))
    BPV = (((2,), (1,)), ((0,), (0,)))
    BTN = (((1,), (1,)), ((0,), (0,)))

    def phase2(c):
        rows = slice(c * CHUNK, (c + 1) * CHUNK)
        v = _heads(v_ref[0, rows, :], GLA_DV)
        sc = lax.dot_general(_heads(qd_ref[rows, :], GLA_DK), _heads(kd_ref[rows, :], GLA_DK), BQK,
                             preferred_element_type=F32)
        sc = jnp.where(tri[None], sc, 0.0).astype(BF16)
        oi = lax.dot_general(sc, v, BPV, preferred_element_type=F32)
        for h in range(GLA_H):
            oi_ref[rows, h * GLA_DV:(h + 1) * GLA_DV] = oi[h]
        kv_ref[c] = lax.dot_general(v, _heads(ks_ref[rows, :], GLA_DK), BTN, preferred_element_type=F32)

    def phase3(c):
        rows = slice(c * CHUNK, (c + 1) * CHUNK)
        st = s_ref[...]
        o = _heads(oi_ref[rows, :], GLA_DV) + lax.dot_general(
            _heads(qs_ref[rows, :], GLA_DK), st.astype(BF16), BQK, preferred_element_type=F32)
        s_ref[...] = st * _heads(eend_ref[c:c + 1, :], GLA_DK) + kv_ref[c]
        o = o * lax.rsqrt(jnp.mean(o * o, axis=-1, keepdims=True) + EPS)
        for h in range(GLA_H):
            vsl = slice(h * GLA_DV, (h + 1) * GLA_DV)
            o_ref[0, rows, vsl] = (o[h] * gn_ref[...] * r_ref[0, rows, vsl].astype(F32)).astype(BF16)

    _skewed(CHUNKS_PER_TILE, (phase1, phase2, phase3))


def _gla(big, la, lamin, gn, tiles_per_batch):
    r = big.shape[0]
    return pl.pallas_call(
        functools.partial(_gla_kernel, tiles_per_batch),
        grid=(r,),
        in_specs=[
            pl.BlockSpec((1, TM, GLA_QK), lambda i: (i, 0, OFF_GQ // GLA_QK)),
            pl.BlockSpec((1, TM, GLA_QK), lambda i: (i, 0, OFF_GK // GLA_QK)),
            pl.BlockSpec((1, TM, GLA_V), lambda i: (i, 0, OFF_GV // GLA_V)),
            pl.BlockSpec((1, TM, GLA_V), lambda i: (i, 0, OFF_GR // GLA_V)),
            pl.BlockSpec((1, TM, GLA_QK), lambda i: (i, 0, 0)),
            pl.BlockSpec((1, 8, 128), lambda i: (i, 0, 0)),
            pl.BlockSpec((1, GLA_DV), lambda i: (0, 0)),
        ],
        out_specs=pl.BlockSpec((1, TM, GLA_V), lambda i: (i, 0, 0)),
        out_shape=jax.ShapeDtypeStruct((r, TM, GLA_V), BF16),
        scratch_shapes=[
            pltpu.VMEM((GLA_H, GLA_DV, GLA_DK), F32),
            pltpu.VMEM((TM, GLA_QK), BF16),
            pltpu.VMEM((TM, GLA_QK), BF16),
            pltpu.VMEM((TM, GLA_QK), BF16),
            pltpu.VMEM((TM, GLA_QK), BF16),
            pltpu.VMEM((16, GLA_QK), F32),
            pltpu.VMEM((TM, GLA_V), F32),
            pltpu.VMEM((CHUNKS_PER_TILE, GLA_H, GLA_DV, GLA_DK), F32),
            pltpu.VMEM((CHUNK, GLA_QK), F32),
            pltpu.VMEM((CHUNK, GLA_QK), F32),
        ],
        compiler_params=_params(("arbitrary",)),
        name="gla",
    )(big, big, big, big, la, lamin, gn)


def _mlstm_kernel(tiles_per_batch, q_ref, k_ref, v_ref, og_ref, mg_ref, mn_ref, o_ref,
                  c_ref, n_ref, m_ref):
    @pl.when(pl.program_id(0) % tiles_per_batch == 0)
    def _():
        c_ref[...] = jnp.zeros_like(c_ref)
        n_ref[...] = jnp.zeros_like(n_ref)
        m_ref[...] = jnp.zeros_like(m_ref)

    BQK = (((2,), (2,)), ((0,), (0,)))
    BPV = (((2,), (1,)), ((0,), (0,)))
    BTN = (((1,), (1,)), ((0,), (0,)))

    def chunk(r0, cs):
        tri = _causal(cs)
        tri_b = tri.astype(BF16)
        rows = slice(r0, r0 + cs)
        sm = mg_ref[0, rows, :]
        fc = _cumsum_rows(tri_b, sm)
        fct = fc.T
        smt = sm.T
        f_col = jnp.stack([fc[:, SMALL_F + h:SMALL_F + h + 1] for h in range(ML_H)], axis=0)
        li_col = jnp.stack([sm[:, SMALL_I + h:SMALL_I + h + 1] for h in range(ML_H)], axis=0)
        f_row = jnp.stack([fct[SMALL_F + h:SMALL_F + h + 1, :] for h in range(ML_H)], axis=0)
        li_row = jnp.stack([smt[SMALL_I + h:SMALL_I + h + 1, :] for h in range(ML_H)], axis=0)
        m_prev = m_ref[:, 0:1, 0:1]
        d_intra = jnp.where(tri[None], f_col - f_row + li_row, -jnp.inf)
        d_inter = f_col + m_prev
        m = jnp.maximum(d_inter, jnp.max(d_intra, axis=-1, keepdims=True))
        w_intra = jnp.exp2(d_intra - m)
        w_inter = jnp.exp2(d_inter - m)
        qh = _heads(q_ref[0, rows, :], ML_DK)
        kh = _heads(k_ref[0, rows, :], ML_DK)
        v = _heads(v_ref[0, rows, :], ML_DV)
        ct = c_ref[...]
        n_prev = n_ref[:, 0:1, :]
        qkw = lax.dot_general(qh, kh, BQK, preferred_element_type=F32) * w_intra
        num = w_inter * lax.dot_general(qh, ct.astype(BF16), BQK, preferred_element_type=F32)
        num = num + lax.dot_general(qkw.astype(BF16), v, BPV, preferred_element_type=F32)
        den = w_inter * jnp.sum(qh.astype(F32) * n_prev, axis=-1, keepdims=True)
        den = den + jnp.sum(qkw, axis=-1, keepdims=True)
        hh = num * (1.0 / jnp.maximum(jnp.abs(den), jnp.exp2(-m)))
        f_last = f_col[:, cs - 1:cs, :]
        d_state = f_last - f_col + li_col
        m_new = jnp.maximum(f_last + m_prev, jnp.max(d_state, axis=1, keepdims=True))
        g_prev = jnp.exp2(f_last + m_prev - m_new)
        kw = kh.astype(F32) * jnp.exp2(d_state - m_new)
        c_ref[...] = g_prev * ct + lax.dot_general(v, kw.astype(BF16), BTN, preferred_element_type=F32)
        n_ref[...] = jnp.broadcast_to(g_prev * n_prev + jnp.sum(kw, axis=1, keepdims=True),
                                      (ML_H, 8, ML_DK))
        m_ref[...] = jnp.broadcast_to(m_new, (ML_H, 8, 128))
        hh = hh * lax.rsqrt(jnp.mean(hh * hh, axis=-1, keepdims=True) + EPS)
        for h in range(ML_H):
            vsl = slice(h * ML_DV, (h + 1) * ML_DV)
            o_ref[0, rows, vsl] = (hh[h] * mn_ref[:, vsl] * og_ref[0, rows, vsl].astype(F32)).astype(BF16)

    r0 = 0
    for cs in ML_CHUNKS:
        chunk(r0, cs)
        r0 += cs


def _mlstm(big, mg, mn, tiles_per_batch):
    r = big.shape[0]
    return pl.pallas_call(
        functools.partial(_mlstm_kernel, tiles_per_batch),
        grid=(r,),
        in_specs=[
            pl.BlockSpec((1, TM, ML_QK), lambda i: (i, 0, OFF_MQ // ML_QK)),
            pl.BlockSpec((1, TM, ML_QK), lambda i: (i, 0, OFF_MK // ML_QK)),
            pl.BlockSpec((1, TM, ML_V), lambda i: (i, 0, OFF_MV // ML_V)),
            pl.BlockSpec((1, TM, ML_V), lambda i: (i, 0, OFF_MO // ML_V)),
            pl.BlockSpec((1, TM, N_SMALL), lambda i: (i, 0, 0)),
            pl.BlockSpec((1, ML_V), lambda i: (0, 0)),
        ],
        out_specs=pl.BlockSpec((1, TM, ML_V), lambda i: (i, 0, 0)),
        out_shape=jax.ShapeDtypeStruct((r, TM, ML_V), BF16),
        scratch_shapes=[
            pltpu.VMEM((ML_H, ML_DV, ML_DK), F32),
            pltpu.VMEM((ML_H, 8, ML_DK), F32),
            pltpu.VMEM((ML_H, 8, 128), F32),
        ],
        compiler_params=_params(("arbitrary",)),
        name="mlstm",
    )(big, big, big, big, mg, mn)


def _merge_kernel(og_ref, om_ref, ga_ref, gb_ref, h_ref, wg_ref, wm_ref, wo_ref, out_ref):
    a = jnp.dot(og_ref[0], wg_ref[...], preferred_element_type=F32)
    b = jnp.dot(om_ref[0], wm_ref[...], preferred_element_type=F32)
    y = ga_ref[0].astype(F32) * a + gb_ref[0].astype(F32) * b
    out_ref[0] = h_ref[0] + jnp.dot(y.astype(BF16), wo_ref[...], preferred_element_type=F32)


def _merge(o_gla, o_ml, big, h, wg, wm, wo):
    r = h.shape[0]
    wspec = pl.BlockSpec((D_MODEL, D_MODEL), lambda i: (0, 0))
    return pl.pallas_call(
        _merge_kernel,
        grid=(r,),
        in_specs=[
            pl.BlockSpec((1, TM, GLA_V), lambda i: (i, 0, 0)),
            pl.BlockSpec((1, TM, ML_V), lambda i: (i, 0, 0)),
            pl.BlockSpec((1, TM, D_MODEL), lambda i: (i, 0, OFF_GA // D_MODEL)),
            pl.BlockSpec((1, TM, D_MODEL), lambda i: (i, 0, OFF_GB // D_MODEL)),
            pl.BlockSpec((1, TM, D_MODEL), lambda i: (i, 0, 0)),
            wspec, wspec, wspec,
        ],
        out_specs=pl.BlockSpec((1, TM, D_MODEL), lambda i: (i, 0, 0)),
        out_shape=jax.ShapeDtypeStruct(h.shape, F32),
        input_output_aliases={4: 0},
        compiler_params=_params(("arbitrary",)),
        name="merge",
    )(o_gla, o_ml, big, big, h, wg, wm, wo)


def _ffn_kernel(tiles_per_batch, h_ref, halo_ref, g_ref, wup_ref, cw_ref, cb_ref, wdn_ref,
                out_ref, xn_ref, a_ref):
    keep = (pl.program_id(0) % tiles_per_batch != 0).astype(F32)
    xn_ref[0:HALO, :] = (_rms(halo_ref[0], g_ref[...]) * keep).astype(BF16)
    xn_ref[HALO:HALO + TM, :] = _rms(h_ref[0], g_ref[...]).astype(BF16)

    def conv_up(cols):
        u = jnp.dot(xn_ref[...], wup_ref[:, cols], preferred_element_type=F32)
        cur = u[HALO:, :]
        y = cb_ref[:, cols] + cw_ref[FFN_CONV - 1:FFN_CONV, cols] * cur
        for tap in range(FFN_CONV - 1):
            k = FFN_CONV - 1 - tap
            y = y + cw_ref[tap:tap + 1, cols] * _time_shift(cur, k, u[HALO - 8 * k:HALO, :])
        return y

    for j in range(N_FF_TILES):
        gate = conv_up(slice(j * FF_TILE, (j + 1) * FF_TILE))
        up = conv_up(slice(D_FF + j * FF_TILE, D_FF + (j + 1) * FF_TILE))
        a_ref[:, j * FF_TILE:(j + 1) * FF_TILE] = (gate * _sigmoid(gate) * up).astype(BF16)

    out_ref[0] = h_ref[0] + jnp.dot(a_ref[...], wdn_ref[...], preferred_element_type=F32)


def _ffn(h, g, wup, cw, cb, wdn, tiles_per_batch):
    r = h.shape[0]
    last_halo_block = TM // HALO - 1
    const = lambda i: (0, 0)
    resident = pl.Buffered(1)
    return pl.pallas_call(
        functools.partial(_ffn_kernel, tiles_per_batch),
        grid=(r,),
        in_specs=[
            pl.BlockSpec((1, TM, D_MODEL), lambda i: (i, 0, 0)),
            pl.BlockSpec((1, HALO, D_MODEL), lambda i: (jnp.maximum(i - 1, 0), last_halo_block, 0)),
            pl.BlockSpec((1, D_MODEL), const),
            pl.BlockSpec((D_MODEL, 2 * D_FF), const, pipeline_mode=resident),
            pl.BlockSpec((FFN_CONV, 2 * D_FF), const),
            pl.BlockSpec((1, 2 * D_FF), const),
            pl.BlockSpec((D_FF, D_MODEL), const, pipeline_mode=resident),
        ],
        out_specs=pl.BlockSpec((1, TM, D_MODEL), lambda i: (i, 0, 0)),
        out_shape=jax.ShapeDtypeStruct(h.shape, F32),
        scratch_shapes=[
            pltpu.VMEM((HALO + TM, D_MODEL), BF16),
            pltpu.VMEM((TM, D_FF), BF16),
        ],
        compiler_params=_params(("arbitrary",)),
        name="ffn",
    )(h, h, g, wup, cw, cb, wdn)


def _final_kernel(a_ref, nxt_ref, g_ref, o_ref):
    y = jnp.concatenate([_block_swap(a_ref[0])[N_META:, :], _block_swap(nxt_ref[0])[:N_META, :]], axis=0)
    o_ref[0] = _rms(y, g_ref[...])


def _final(h, g, seq):
    b, lp, _ = h.shape
    blocks_per_tile = TM // BLOCK
    last_block = lp // BLOCK - 1
    return pl.pallas_call(
        _final_kernel,
        grid=(b, lp // TM),
        in_specs=[
            pl.BlockSpec((1, TM, D_MODEL), lambda i, j: (i, j, 0)),
            pl.BlockSpec((1, BLOCK, D_MODEL),
                         lambda i, j: (i, jnp.minimum((j + 1) * blocks_per_tile, last_block), 0)),
            pl.BlockSpec((1, D_MODEL), lambda i, j: (0, 0)),
        ],
        out_specs=pl.BlockSpec((1, TM, D_MODEL), lambda i, j: (i, j, 0)),
        out_shape=jax.ShapeDtypeStruct((b, seq, D_MODEL), F32),
        compiler_params=_params(("arbitrary", "arbitrary")),
        name="final_norm",
    )(h, h, g)


def kernel(x, meta, norm_mix, w_in, b_in, gla_w_gate, gla_b_gate, gla_norm, ml_conv_w, ml_conv_b,
           ml_norm, w_branch_gla, w_branch_ml, w_out, norm_ffn, ffn_w_up, ffn_conv_w, ffn_conv_b,
           ffn_w_down, norm_final):
    b, seq, _ = x.shape
    depth = w_in.shape[0]
    lp = -(-(N_META + seq) // TM) * TM
    tiles_per_batch = lp // TM
    assert lp % CHUNK == 0

    h = jnp.concatenate([
        jnp.broadcast_to(meta[None].astype(x.dtype), (b, N_META, D_MODEL)), x,
        jnp.zeros((b, lp - N_META - seq, D_MODEL), x.dtype)], axis=1)
    h = jnp.swapaxes(h.reshape(b, lp // BLOCK, 8, 8, D_MODEL), 2, 3)
    h = h.reshape(b * tiles_per_batch, TM, D_MODEL)

    c_gr = 2 * GLA_QK + GLA_V
    c_lr = c_gr + GLA_V
    c_mq = c_lr + GLA_LR
    c_mo = c_mq + 2 * ML_QK + ML_V
    c_mi = c_mo + ML_V
    c_ga = c_mi + 2 * ML_H

    def regroup(t):
        big = jnp.concatenate([t[..., :c_gr], t[..., c_mq:c_mo], t[..., c_gr:c_lr],
                               t[..., c_mo:c_mi], t[..., c_ga:]], axis=-1)
        small = jnp.concatenate([t[..., c_lr:c_mq], t[..., c_mi:c_ga]], axis=-1)
        pad = [(0, 0)] * (t.ndim - 1) + [(0, N_SMALL - small.shape[-1])]
        return big, jnp.pad(small, pad)

    w_big, w_small = regroup(w_in)
    w_big, w_small = w_big.astype(BF16), w_small.astype(BF16)
    b_big, b_small = regroup(b_in[:, None, :])
    wg_gate = jnp.pad(gla_w_gate, ((0, 0), (0, N_SMALL - GLA_LR), (0, 0))).astype(BF16)
    wup = ffn_w_up.astype(BF16)
    wdn = ffn_w_down.astype(BF16)
    wbg = w_branch_gla.astype(BF16)
    wbm = w_branch_ml.astype(BF16)
    wo = w_out.astype(BF16)

    for l in range(depth):
        big, la, mg, lamin = _inproj(h, norm_mix[l][None], w_big[l], b_big[l], w_small[l], b_small[l],
                                     wg_gate[l], gla_b_gate[l][None], ml_conv_w[l],
                                     ml_conv_b[l][None], tiles_per_batch)
        o_gla = _gla(big, la, lamin, gla_norm[l][None], tiles_per_batch)
        o_ml = _mlstm(big, mg, ml_norm[l][None], tiles_per_batch)
        h = _merge(o_gla, o_ml, big, h, wbg[l], wbm[l], wo[l])
        h = _ffn(h, norm_ffn[l][None], wup[l], ffn_conv_w[l], ffn_conv_b[l][None], wdn[l],
                 tiles_per_batch)

    return _final(h.reshape(b, lp, D_MODEL), norm_final[None], seq)
```

```python
import functools
import math

import jax
import jax.numpy as jnp
from jax import lax
from jax.experimental import pallas as pl
from jax.experimental.pallas import tpu as pltpu

F32 = jnp.float32
BF16 = jnp.bfloat16

D_MODEL = 1024
N_META = 16
CHUNK = 64
BLOCK = 64
GLA_H = 4
GLA_DK = 128
GLA_DV = 256
GLA_LR = 16
GLA_TAU = 16.0
ML_H = 4
ML_DK = 128
ML_DV = 256
ML_CONV = 4
D_FF = 2816
FFN_CONV = 3
EPS = 1e-6
LOG2E = math.log2(math.e)

GLA_QK = GLA_H * GLA_DK
GLA_V = GLA_H * GLA_DV
ML_QK = ML_H * ML_DK
ML_V = ML_H * ML_DV

TM = 832
CHUNKS_PER_TILE = TM // CHUNK

INPROJ_TN = 2048
N_BIG = 4 * INPROJ_TN
OFF_GQ, OFF_GK, OFF_GV = 0, 512, 1024
OFF_MQ, OFF_MK, OFF_MV = 2048, 2560, 3072
OFF_GR, OFF_MO = 4096, 5120
OFF_GA, OFF_GB = 6144, 7168
N_SMALL = 128
SMALL_I = 16
SMALL_F = 20

FF_TILE = 256
N_FF_TILES = D_FF // FF_TILE
HALO = 16
VMEM_LIMIT = 56 * 1024 * 1024
ML_CHUNKS = (320, 256, 256)
CONV_CARRY = 8 * (ML_CONV - 1)
GLA_SAFE_LOG2_DECAY = -3.0

NT_DIMS = (((1,), (1,)), ((), ()))
TN_DIMS = (((0,), (0,)), ((), ()))


def _rms(x, g):
    return x * lax.rsqrt(jnp.mean(x * x, axis=-1, keepdims=True) + EPS) * g


def _sigmoid(x):
    return 0.5 * jnp.tanh(0.5 * x) + 0.5


def _log2_sigmoid(x):
    x2 = x * LOG2E
    return jnp.minimum(x2, 0.0) - jnp.log2(1.0 + jnp.exp2(-jnp.abs(x2)))


def _row_time(r):
    blk = jnp.bitwise_and(r, -BLOCK)
    return blk + jnp.left_shift(jnp.bitwise_and(r, 7), 3) + jnp.bitwise_and(jnp.right_shift(r, 3), 7)


def _row_of_time(t):
    return (t // BLOCK) * BLOCK + (t % 8) * 8 + (t // 8) % 8


def _causal(n):
    t_r = _row_time(lax.broadcasted_iota(jnp.int32, (n, n), 0))
    t_c = _row_time(lax.broadcasted_iota(jnp.int32, (n, n), 1))
    return t_r >= t_c


def _block_swap(x):
    n, c = x.shape
    return jnp.swapaxes(x.reshape(n // BLOCK, 8, 8, c), 1, 2).reshape(n, c)


def _time_shift(x, k, prev_tail):
    n, c = x.shape
    x4 = x.reshape(n // BLOCK, 8, 8, c)
    tail = x4[:, 8 - k:]
    prev = jnp.concatenate([prev_tail.reshape(1, k, 8, c), tail[:-1]], axis=0)
    sub = lax.broadcasted_iota(jnp.int32, tail.shape, 2)
    wrapped = jnp.where(sub == 0, pltpu.roll(prev, 1, axis=2), pltpu.roll(tail, 1, axis=2))
    return jnp.concatenate([wrapped, x4[:, :8 - k]], axis=1).reshape(n, c)


def _cumsum_rows(tri_b, x):
    hi = x.astype(BF16)
    lo = (x - hi.astype(F32)).astype(BF16)
    return (jnp.dot(tri_b, hi, preferred_element_type=F32)
            + jnp.dot(tri_b, lo, preferred_element_type=F32))


def _heads(x, d):
    return jnp.stack([x[:, h * d:(h + 1) * d] for h in range(GLA_H)], axis=0)


def _params(sem):
    return pltpu.CompilerParams(dimension_semantics=sem, vmem_limit_bytes=VMEM_LIMIT)


def _skewed(n, phases):
    for i in range(n + len(phases) - 1):
        for k, phase in enumerate(phases):
            if 0 <= i - k < n:
                phase(i - k)


def _prep_kernel(seq, meta_ref, x_ref, xprev_ref, o_ref):
    j = pl.program_id(1)
    head = jnp.where(j == 0, meta_ref[...], xprev_ref[0])
    nat = jnp.concatenate([head, x_ref[0, :TM - N_META, :]], axis=0)
    row = j * TM + lax.broadcasted_iota(jnp.int32, (TM, 1), 0)
    nat = jnp.where(row < N_META + seq, nat, 0.0)
    o_ref[0] = _block_swap(nat)


def _prep(x, meta, lp):
    b, seq, _ = x.shape
    tiles_per_batch = lp // TM
    meta_blocks_per_tile = TM // N_META
    return pl.pallas_call(
        functools.partial(_prep_kernel, seq),
        grid=(b, tiles_per_batch),
        in_specs=[
            pl.BlockSpec((N_META, D_MODEL), lambda i, j: (0, 0)),
            pl.BlockSpec((1, TM, D_MODEL), lambda i, j: (i, j, 0)),
            pl.BlockSpec((1, N_META, D_MODEL),
                         lambda i, j: (i, jnp.maximum(j * meta_blocks_per_tile - 1, 0), 0)),
        ],
        out_specs=pl.BlockSpec((1, TM, D_MODEL), lambda i, j: (i * tiles_per_batch + j, 0, 0)),
        out_shape=jax.ShapeDtypeStruct((b * tiles_per_batch, TM, D_MODEL), F32),
        compiler_params=_params(("arbitrary", "arbitrary")),
        name="prep",
    )(meta, x, x)


def _inproj_kernel(tiles_per_batch, h_ref, g_ref, w_ref, b_ref, ws_ref, bs_ref, wg_ref, bg_ref,
                   cw_ref, cb_ref, big_ref, la_ref, mg_ref, lamin_ref, xn_ref, carry_ref):
    j = pl.program_id(1)

    def project():
        return jnp.dot(xn_ref[...], w_ref[...], preferred_element_type=F32) + b_ref[...]

    @pl.when(j == 0)
    def _():
        xn = _rms(h_ref[0], g_ref[...]).astype(BF16)
        xn_ref[...] = xn
        small = jnp.dot(xn, ws_ref[...], preferred_element_type=F32) + bs_ref[...]
        x = jnp.dot(small.astype(BF16), wg_ref[...], preferred_element_type=F32) + bg_ref[...]
        la = _log2_sigmoid(x) * (1.0 / GLA_TAU)
        la_ref[0] = la
        lmin = jnp.min(la.reshape(TM // 8, 8, GLA_QK), axis=0)
        lamin_ref[0] = functools.reduce(
            jnp.minimum, [lmin[:, i * 128:(i + 1) * 128] for i in range(GLA_QK // 128)])
        lane = lax.broadcasted_iota(jnp.int32, small.shape, 1)
        is_f = jnp.logical_and(lane >= SMALL_F, lane < SMALL_F + ML_H)
        mg_ref[0] = jnp.where(is_f, _log2_sigmoid(small), small * LOG2E)
        acc = project()
        big_ref[0, :, :GLA_QK] = (acc[:, :GLA_QK] * (GLA_DK ** -0.5)).astype(BF16)
        big_ref[0, :, GLA_QK:] = acc[:, GLA_QK:].astype(BF16)

    @pl.when(j == 1)
    def _():
        @pl.when(pl.program_id(0) % tiles_per_batch == 0)
        def _():
            carry_ref[...] = jnp.zeros_like(carry_ref)

        acc = project()
        pre = acc[:, :2 * ML_QK]
        y = cb_ref[...] + cw_ref[ML_CONV - 1:ML_CONV, :] * pre
        for tap in range(ML_CONV - 1):
            k = ML_CONV - 1 - tap
            y = y + cw_ref[tap:tap + 1, :] * _time_shift(pre, k, carry_ref[CONV_CARRY - 8 * k:, :])
        carry_ref[...] = pre[TM - CONV_CARRY:, :]
        qk = y * _sigmoid(y)
        big_ref[0, :, :ML_QK] = (qk[:, :ML_QK] * (ML_DK ** -0.5)).astype(BF16)
        big_ref[0, :, ML_QK:2 * ML_QK] = qk[:, ML_QK:].astype(BF16)
        big_ref[0, :, 2 * ML_QK:] = acc[:, 2 * ML_QK:].astype(BF16)

    @pl.when(j == 2)
    def _():
        acc = project()
        r = acc[:, :GLA_V]
        big_ref[0, :, :GLA_V] = (r * _sigmoid(r)).astype(BF16)
        big_ref[0, :, GLA_V:] = _sigmoid(acc[:, GLA_V:]).astype(BF16)

    @pl.when(j == 3)
    def _():
        big_ref[0] = _sigmoid(project()).astype(BF16)


def _inproj(h, g, w, b, ws, bs, wg, bg, cw, cb, tiles_per_batch):
    r = h.shape[0]
    const = lambda i, j: (0, 0)
    return pl.pallas_call(
        functools.partial(_inproj_kernel, tiles_per_batch),
        grid=(r, N_BIG // INPROJ_TN),
        in_specs=[
            pl.BlockSpec((1, TM, D_MODEL), lambda i, j: (i, 0, 0)),
            pl.BlockSpec((1, D_MODEL), const),
            pl.BlockSpec((D_MODEL, INPROJ_TN), lambda i, j: (0, j)),
            pl.BlockSpec((1, INPROJ_TN), lambda i, j: (0, j)),
            pl.BlockSpec((D_MODEL, N_SMALL), const),
            pl.BlockSpec((1, N_SMALL), const),
            pl.BlockSpec((N_SMALL, GLA_QK), const),
            pl.BlockSpec((1, GLA_QK), const),
            pl.BlockSpec((ML_CONV, 2 * ML_QK), const),
            pl.BlockSpec((1, 2 * ML_QK), const),
        ],
        out_specs=[
            pl.BlockSpec((1, TM, INPROJ_TN), lambda i, j: (i, 0, j)),
            pl.BlockSpec((1, TM, GLA_QK), lambda i, j: (i, 0, 0)),
            pl.BlockSpec((1, TM, N_SMALL), lambda i, j: (i, 0, 0)),
            pl.BlockSpec((1, 8, 128), lambda i, j: (i, 0, 0)),
        ],
        out_shape=[
            jax.ShapeDtypeStruct((r, TM, N_BIG), BF16),
            jax.ShapeDtypeStruct((r, TM, GLA_QK), F32),
            jax.ShapeDtypeStruct((r, TM, N_SMALL), F32),
            jax.ShapeDtypeStruct((r, 8, 128), F32),
        ],
        scratch_shapes=[
            pltpu.VMEM((TM, D_MODEL), BF16),
            pltpu.VMEM((CONV_CARRY, 2 * ML_QK), F32),
        ],
        compiler_params=_params(("arbitrary", "arbitrary")),
        name="inproj",
    )(h, g, w, b, ws, bs, wg, bg, cw, cb)


def _gla_kernel(tiles_per_batch, q_ref, k_ref, v_ref, r_ref, la_ref, lamin_ref, gn_ref,
                o_ref, s_ref, qs_ref, qd_ref, kd_ref, ks_ref, eend_ref, oi_ref, kv_ref, bc_ref, qf_ref):
    @pl.when(pl.program_id(0) % tiles_per_batch == 0)
    def _():
        s_ref[...] = jnp.zeros_like(s_ref)

    safe = jnp.min(lamin_ref[0]) >= GLA_SAFE_LOG2_DECAY

    @pl.when(safe)
    def _():
        _gla_factored(q_ref, k_ref, v_ref, r_ref, la_ref, gn_ref, o_ref, s_ref, qs_ref, qd_ref,
                      kd_ref, ks_ref, eend_ref, oi_ref, kv_ref)

    @pl.when(jnp.logical_not(safe))
    def _():
        _gla_direct(q_ref, k_ref, v_ref, r_ref, la_ref, gn_ref, o_ref, s_ref, oi_ref, bc_ref, qf_ref)


def _gla_direct(q_ref, k_ref, v_ref, r_ref, la_ref, gn_ref, o_ref, s_ref, oi_ref, bc_ref, qf_ref):
    tri_b = _causal(CHUNK).astype(BF16)
    s_time = _row_time(lax.broadcasted_iota(jnp.int32, (CHUNK, 1), 0))

    def chunk(c, carry):
        r0 = pl.multiple_of(c * CHUNK, CHUNK)
        rows = pl.ds(r0, CHUNK)
        bc = _cumsum_rows(tri_b, la_ref[0, rows, :])
        bend = bc[CHUNK - 1:CHUNK, :]
        q = q_ref[0, rows, :].astype(F32)
        k = k_ref[0, rows, :].astype(F32)
        bc_ref[...] = bc
        qf_ref[...] = q
        qs = (q * jnp.exp2(bc)).astype(BF16)
        ks = (k * jnp.exp2(bend - bc)).astype(BF16)
        e_end = jnp.exp2(bend)
        for h in range(GLA_H):
            ksl = slice(h * GLA_DK, (h + 1) * GLA_DK)
            vsl = slice(h * GLA_DV, (h + 1) * GLA_DV)
            kh = k[:, ksl]
            bch = bc[:, ksl]
            vh = v_ref[0, rows, vsl].astype(F32)

            def group(g, carry2):
                g8 = pl.multiple_of(g * 8, 8)
                bt8 = bc_ref[pl.ds(g8, 8), ksl]
                qt8 = qf_ref[pl.ds(g8, 8), ksl]
                out_rows = []
                for j in range(8):
                    e = jnp.exp2(jnp.minimum(bt8[j:j + 1, :] - bch, 0.0))
                    col = jnp.sum(e * kh * qt8[j:j + 1, :], axis=-1, keepdims=True)
                    col = jnp.where(s_time <= 8 * j + g, col, 0.0)
                    out_rows.append(jnp.sum(col * vh, axis=0, keepdims=True))
                oi_ref[pl.ds(r0 + g8, 8), vsl] = jnp.concatenate(out_rows, axis=0)
                return carry2

            lax.fori_loop(0, CHUNK // 8, group, 0)
            st = s_ref[h]
            o = oi_ref[rows, vsl] + lax.dot_general(qs[:, ksl], st.astype(BF16), NT_DIMS,
                                                    preferred_element_type=F32)
            s_ref[h] = st * e_end[:, ksl] + lax.dot_general(
                v_ref[0, rows, vsl], ks[:, ksl], TN_DIMS, preferred_element_type=F32)
            o = _rms(o, gn_ref[...]) * r_ref[0, rows, vsl].astype(F32)
            o_ref[0, rows, vsl] = o.astype(BF16)
        return carry

    lax.fori_loop(0, CHUNKS_PER_TILE, chunk, 0)


def _gla_factored(q_ref, k_ref, v_ref, r_ref, la_ref, gn_ref, o_ref, s_ref, qs_ref, qd_ref,
                  kd_ref, ks_ref, eend_ref, oi_ref, kv_ref):
    tri = _causal(CHUNK)
    tri_b = tri.astype(BF16)
    mid_row = _row_of_time(CHUNK // 2 - 1)

    def phase1(c):
        rows = slice(c * CHUNK, (c + 1) * CHUNK)
        bc = _cumsum_rows(tri_b, la_ref[0, rows, :])
        ref = bc[mid_row:mid_row + 1, :]
        bend = bc[CHUNK - 1:CHUNK, :]
        e_fwd = jnp.exp2(bc - ref)
        e_bwd = jnp.exp2(ref - bc)
        q = q_ref[0, rows, :]
        k = k_ref[0, rows, :]
        qd_ref[rows, :] = q * e_fwd.astype(BF16)
        qs_ref[rows, :] = q * (e_fwd * jnp.exp2(ref)).astype(BF16)
        kd_ref[rows, :] = k * e_bwd.astype(BF16)
        ks_ref[rows, :] = k * (e_bwd * jnp.exp2(bend - ref)).astype(BF16)
        eend_ref[c:c + 1, :] = jnp.exp2(bend)

    BQK = (((2,), (2,)), ((0,), (0,)))
    BPV = (((2,), (1,)), ((0,), (0,)))
    BTN = (((1,), (1,)), ((0,), (0,)))

    def phase2(c):
        rows = slice(c * CHUNK, (c + 1) * CHUNK)
        v = _heads(v_ref[0, rows, :], GLA_DV)
        sc = lax.dot_general(_heads(qd_ref[rows, :], GLA_DK), _heads(kd_ref[rows, :], GLA_DK), BQK,
                             preferred_element_type=F32)
        sc = jnp.where(tri[None], sc, 0.0).astype(BF16)
        oi = lax.dot_general(sc, v, BPV, preferred_element_type=F32)
        for h in range(GLA_H):
            oi_ref[rows, h * GLA_DV:(h + 1) * GLA_DV] = oi[h]
        kv_ref[c] = lax.dot_general(v, _heads(ks_ref[rows, :], GLA_DK), BTN, preferred_element_type=F32)

    def phase3(c):
        rows = slice(c * CHUNK, (c + 1) * CHUNK)
        st = s_ref[...]
        o = _heads(oi_ref[rows, :], GLA_DV) + lax.dot_general(
            _heads(qs_ref[rows, :], GLA_DK), st.astype(BF16), BQK, preferred_element_type=F32)
        s_ref[...] = st * _heads(eend_ref[c:c + 1, :], GLA_DK) + kv_ref[c]
        o = o * lax.rsqrt(jnp.mean(o * o, axis=-1, keepdims=True) + EPS)
        for h in range(GLA_H):
            vsl = slice(h * GLA_DV, (h + 1) * GLA_DV)
            o_ref[0, rows, vsl] = (o[h] * gn_ref[...] * r_ref[0, rows, vsl].astype(F32)).astype(BF16)

    _skewed(CHUNKS_PER_TILE, (phase1, phase2, phase3))


def _gla(big, la, lamin, gn, tiles_per_batch):
    r = big.shape[0]
    return pl.pallas_call(
        functools.partial(_gla_kernel, tiles_per_batch),
        grid=(r,),
        in_specs=[
            pl.BlockSpec((1, TM, GLA_QK), lambda i: (i, 0, OFF_GQ // GLA_QK)),
            pl.BlockSpec((1, TM, GLA_QK), lambda i: (i, 0, OFF_GK // GLA_QK)),
            pl.BlockSpec((1, TM, GLA_V), lambda i: (i, 0, OFF_GV // GLA_V)),
            pl.BlockSpec((1, TM, GLA_V), lambda i: (i, 0, OFF_GR // GLA_V)),
            pl.BlockSpec((1, TM, GLA_QK), lambda i: (i, 0, 0)),
            pl.BlockSpec((1, 8, 128), lambda i: (i, 0, 0)),
            pl.BlockSpec((1, GLA_DV), lambda i: (0, 0)),
        ],
        out_specs=pl.BlockSpec((1, TM, GLA_V), lambda i: (i, 0, 0)),
        out_shape=jax.ShapeDtypeStruct((r, TM, GLA_V), BF16),
        scratch_shapes=[
            pltpu.VMEM((GLA_H, GLA_DV, GLA_DK), F32),
            pltpu.VMEM((TM, GLA_QK), BF16),
            pltpu.VMEM((TM, GLA_QK), BF16),
            pltpu.VMEM((TM, GLA_QK), BF16),
            pltpu.VMEM((TM, GLA_QK), BF16),
            pltpu.VMEM((16, GLA_QK), F32),
            pltpu.VMEM((TM, GLA_V), F32),
            pltpu.VMEM((CHUNKS_PER_TILE, GLA_H, GLA_DV, GLA_DK), F32),
            pltpu.VMEM((CHUNK, GLA_QK), F32),
            pltpu.VMEM((CHUNK, GLA_QK), F32),
        ],
        compiler_params=_params(("arbitrary",)),
        name="gla",
    )(big, big, big, big, la, lamin, gn)


def _mlstm_kernel(tiles_per_batch, q_ref, k_ref, v_ref, og_ref, mg_ref, mn_ref, o_ref,
                  c_ref, n_ref, m_ref):
    @pl.when(pl.program_id(0) % tiles_per_batch == 0)
    def _():
        c_ref[...] = jnp.zeros_like(c_ref)
        n_ref[...] = jnp.zeros_like(n_ref)
        m_ref[...] = jnp.zeros_like(m_ref)

    BQK = (((2,), (2,)), ((0,), (0,)))
    BPV = (((2,), (1,)), ((0,), (0,)))
    BTN = (((1,), (1,)), ((0,), (0,)))

    def lanes(x, n):
        reps = -(-n // 128)
        return jnp.concatenate([x] * reps, axis=-1)[:, :, :n]

    def chunk(r0, cs):
        tri = _causal(cs)
        tri_b = tri.astype(BF16)
        rows = slice(r0, r0 + cs)
        sm = mg_ref[0, rows, :]
        fc = _cumsum_rows(tri_b, sm)
        fct = fc.T
        smt = sm.T
        f_col = jnp.stack([fc[:, SMALL_F + h:SMALL_F + h + 1] for h in range(ML_H)], axis=0)
        li_col = jnp.stack([sm[:, SMALL_I + h:SMALL_I + h + 1] for h in range(ML_H)], axis=0)
        f_row = jnp.stack([fct[SMALL_F + h:SMALL_F + h + 1, :] for h in range(ML_H)], axis=0)
        li_row = jnp.stack([smt[SMALL_I + h:SMALL_I + h + 1, :] for h in range(ML_H)], axis=0)
        m_prev = m_ref[:, 0:1, :]
        f_rep = f_col + jnp.zeros((ML_H, cs, 128), F32)
        d_intra = jnp.where(tri[None], lanes(f_rep, cs) - f_row + li_row, -jnp.inf)
        d_inter = f_rep + m_prev
        a_rep = jnp.max(d_intra, axis=-1, keepdims=True) + jnp.zeros((ML_H, cs, 128), F32)
        m = jnp.maximum(d_inter, a_rep)
        w_intra = jnp.exp2(d_intra - lanes(m, cs))
        w_inter = jnp.exp2(d_inter - m)
        qh = _heads(q_ref[0, rows, :], ML_DK)
        kh = _heads(k_ref[0, rows, :], ML_DK)
        v = _heads(v_ref[0, rows, :], ML_DV)
        ct = c_ref[...]
        n_prev = n_ref[:, 0:1, :]
        qkw = (lax.dot_general(qh, kh, BQK, preferred_element_type=F32) * w_intra).astype(BF16)
        num = lanes(w_inter, ML_DV) * lax.dot_general(qh, ct.astype(BF16), BQK, preferred_element_type=F32)
        num = num + lax.dot_general(qkw, v, BPV, preferred_element_type=F32)
        n_rows = jnp.concatenate([n_ref[...]] * (128 // 8), axis=1).astype(BF16)
        qn = lax.dot_general(qh, n_rows, BQK, preferred_element_type=F32)
        rowsum = lax.dot_general(qkw, jnp.ones((ML_H, cs, 128), BF16), BPV, preferred_element_type=F32)
        den = w_inter * qn + rowsum
        hh = num * lanes(1.0 / jnp.maximum(jnp.abs(den), jnp.exp2(-m)), ML_DV)
        f_last = f_col[:, cs - 1:cs, :]
        d_state = f_last - f_col + li_col
        m_new = jnp.maximum(f_last + m_prev, jnp.max(d_state, axis=1, keepdims=True))
        g_prev = jnp.exp2(f_last + m_prev - m_new)
        kw = kh.astype(F32) * jnp.exp2((d_state + jnp.zeros((ML_H, cs, 128), F32)) - m_new)
        c_ref[...] = g_prev * ct + lax.dot_general(v, kw.astype(BF16), BTN, preferred_element_type=F32)
        n_ref[...] = jnp.broadcast_to(g_prev * n_prev + jnp.sum(kw, axis=1, keepdims=True),
                                      (ML_H, 8, ML_DK))
        m_ref[...] = jnp.broadcast_to(m_new, (ML_H, 8, 128))
        ms = lax.dot_general((hh * hh).astype(BF16), jnp.ones((ML_H, ML_DV, 128), BF16), BPV,
                             preferred_element_type=F32) * (1.0 / ML_DV)
        hh = hh * lanes(lax.rsqrt(ms + EPS), ML_DV)
        for h in range(ML_H):
            vsl = slice(h * ML_DV, (h + 1) * ML_DV)
            o_ref[0, rows, vsl] = (hh[h] * mn_ref[:, vsl] * og_ref[0, rows, vsl].astype(F32)).astype(BF16)

    r0 = 0
    for cs in ML_CHUNKS:
        chunk(r0, cs)
        r0 += cs


def _mlstm(big, mg, mn, tiles_per_batch):
    r = big.shape[0]
    return pl.pallas_call(
        functools.partial(_mlstm_kernel, tiles_per_batch),
        grid=(r,),
        in_specs=[
            pl.BlockSpec((1, TM, ML_QK), lambda i: (i, 0, OFF_MQ // ML_QK)),
            pl.BlockSpec((1, TM, ML_QK), lambda i: (i, 0, OFF_MK // ML_QK)),
            pl.BlockSpec((1, TM, ML_V), lambda i: (i, 0, OFF_MV // ML_V)),
            pl.BlockSpec((1, TM, ML_V), lambda i: (i, 0, OFF_MO // ML_V)),
            pl.BlockSpec((1, TM, N_SMALL), lambda i: (i, 0, 0)),
            pl.BlockSpec((1, ML_V), lambda i: (0, 0)),
        ],
        out_specs=pl.BlockSpec((1, TM, ML_V), lambda i: (i, 0, 0)),
        out_shape=jax.ShapeDtypeStruct((r, TM, ML_V), BF16),
        scratch_shapes=[
            pltpu.VMEM((ML_H, ML_DV, ML_DK), F32),
            pltpu.VMEM((ML_H, 8, ML_DK), F32),
            pltpu.VMEM((ML_H, 8, 128), F32),
        ],
        compiler_params=_params(("arbitrary",)),
        name="mlstm",
    )(big, big, big, big, mg, mn)


def _merge_kernel(og_ref, om_ref, ga_ref, gb_ref, h_ref, wg_ref, wm_ref, wo_ref, out_ref):
    a = jnp.dot(og_ref[0], wg_ref[...], preferred_element_type=F32)
    b = jnp.dot(om_ref[0], wm_ref[...], preferred_element_type=F32)
    y = ga_ref[0].astype(F32) * a + gb_ref[0].astype(F32) * b
    out_ref[0] = h_ref[0] + jnp.dot(y.astype(BF16), wo_ref[...], preferred_element_type=F32)


def _merge(o_gla, o_ml, big, h, wg, wm, wo):
    r = h.shape[0]
    wspec = pl.BlockSpec((D_MODEL, D_MODEL), lambda i: (0, 0))
    return pl.pallas_call(
        _merge_kernel,
        grid=(r,),
        in_specs=[
            pl.BlockSpec((1, TM, GLA_V), lambda i: (i, 0, 0)),
            pl.BlockSpec((1, TM, ML_V), lambda i: (i, 0, 0)),
            pl.BlockSpec((1, TM, D_MODEL), lambda i: (i, 0, OFF_GA // D_MODEL)),
            pl.BlockSpec((1, TM, D_MODEL), lambda i: (i, 0, OFF_GB // D_MODEL)),
            pl.BlockSpec((1, TM, D_MODEL), lambda i: (i, 0, 0)),
            wspec, wspec, wspec,
        ],
        out_specs=pl.BlockSpec((1, TM, D_MODEL), lambda i: (i, 0, 0)),
        out_shape=jax.ShapeDtypeStruct(h.shape, F32),
        input_output_aliases={4: 0},
        compiler_params=_params(("arbitrary",)),
        name="merge",
    )(o_gla, o_ml, big, big, h, wg, wm, wo)


def _ffn_kernel(tiles_per_batch, h_ref, halo_ref, g_ref, wup_ref, cw_ref, cb_ref, wdn_ref,
                out_ref, xn_ref, a_ref):
    keep = (pl.program_id(0) % tiles_per_batch != 0).astype(F32)
    xn_ref[0:HALO, :] = (_rms(halo_ref[0], g_ref[...]) * keep).astype(BF16)
    xn_ref[HALO:HALO + TM, :] = _rms(h_ref[0], g_ref[...]).astype(BF16)

    def conv_up(cols):
        u = jnp.dot(xn_ref[...], wup_ref[:, cols], preferred_element_type=F32)
        cur = u[HALO:, :]
        y = cb_ref[:, cols] + cw_ref[FFN_CONV - 1:FFN_CONV, cols] * cur
        for tap in range(FFN_CONV - 1):
            k = FFN_CONV - 1 - tap
            y = y + cw_ref[tap:tap + 1, cols] * _time_shift(cur, k, u[HALO - 8 * k:HALO, :])
        return y

    for j in range(N_FF_TILES):
        gate = conv_up(slice(j * FF_TILE, (j + 1) * FF_TILE))
        up = conv_up(slice(D_FF + j * FF_TILE, D_FF + (j + 1) * FF_TILE))
        a_ref[:, j * FF_TILE:(j + 1) * FF_TILE] = (gate * _sigmoid(gate) * up).astype(BF16)

    out_ref[0] = h_ref[0] + jnp.dot(a_ref[...], wdn_ref[...], preferred_element_type=F32)


def _ffn(h, g, wup, cw, cb, wdn, tiles_per_batch):
    r = h.shape[0]
    last_halo_block = TM // HALO - 1
    const = lambda i: (0, 0)
    resident = pl.Buffered(1)
    return pl.pallas_call(
        functools.partial(_ffn_kernel, tiles_per_batch),
        grid=(r,),
        in_specs=[
            pl.BlockSpec((1, TM, D_MODEL), lambda i: (i, 0, 0)),
            pl.BlockSpec((1, HALO, D_MODEL), lambda i: (jnp.maximum(i - 1, 0), last_halo_block, 0)),
            pl.BlockSpec((1, D_MODEL), const),
            pl.BlockSpec((D_MODEL, 2 * D_FF), const, pipeline_mode=resident),
            pl.BlockSpec((FFN_CONV, 2 * D_FF), const),
            pl.BlockSpec((1, 2 * D_FF), const),
            pl.BlockSpec((D_FF, D_MODEL), const, pipeline_mode=resident),
        ],
        out_specs=pl.BlockSpec((1, TM, D_MODEL), lambda i: (i, 0, 0)),
        out_shape=jax.ShapeDtypeStruct(h.shape, F32),
        scratch_shapes=[
            pltpu.VMEM((HALO + TM, D_MODEL), BF16),
            pltpu.VMEM((TM, D_FF), BF16),
        ],
        compiler_params=_params(("arbitrary",)),
        name="ffn",
    )(h, h, g, wup, cw, cb, wdn)


def _final_kernel(a_ref, nxt_ref, g_ref, o_ref):
    y = jnp.concatenate([_block_swap(a_ref[0])[N_META:, :], _block_swap(nxt_ref[0])[:N_META, :]], axis=0)
    o_ref[0] = _rms(y, g_ref[...])


def _final(h, g, seq):
    b, lp, _ = h.shape
    blocks_per_tile = TM // BLOCK
    last_block = lp // BLOCK - 1
    return pl.pallas_call(
        _final_kernel,
        grid=(b, lp // TM),
        in_specs=[
            pl.BlockSpec((1, TM, D_MODEL), lambda i, j: (i, j, 0)),
            pl.BlockSpec((1, BLOCK, D_MODEL),
                         lambda i, j: (i, jnp.minimum((j + 1) * blocks_per_tile, last_block), 0)),
            pl.BlockSpec((1, D_MODEL), lambda i, j: (0, 0)),
        ],
        out_specs=pl.BlockSpec((1, TM, D_MODEL), lambda i, j: (i, j, 0)),
        out_shape=jax.ShapeDtypeStruct((b, seq, D_MODEL), F32),
        compiler_params=_params(("arbitrary", "arbitrary")),
        name="final_norm",
    )(h, h, g)


def kernel(x, meta, norm_mix, w_in, b_in, gla_w_gate, gla_b_gate, gla_norm, ml_conv_w, ml_conv_b,
           ml_norm, w_branch_gla, w_branch_ml, w_out, norm_ffn, ffn_w_up, ffn_conv_w, ffn_conv_b,
           ffn_w_down, norm_final):
    b, seq, _ = x.shape
    depth = w_in.shape[0]
    lp = -(-(N_META + seq) // TM) * TM
    tiles_per_batch = lp // TM
    assert lp % CHUNK == 0

    h = _prep(x, meta.astype(x.dtype), lp)

    c_gr = 2 * GLA_QK + GLA_V
    c_lr = c_gr + GLA_V
    c_mq = c_lr + GLA_LR
    c_mo = c_mq + 2 * ML_QK + ML_V
    c_mi = c_mo + ML_V
    c_ga = c_mi + 2 * ML_H

    def regroup(t):
        big = jnp.concatenate([t[..., :c_gr], t[..., c_mq:c_mo], t[..., c_gr:c_lr],
                               t[..., c_mo:c_mi], t[..., c_ga:]], axis=-1)
        small = jnp.concatenate([t[..., c_lr:c_mq], t[..., c_mi:c_ga]], axis=-1)
        pad = [(0, 0)] * (t.ndim - 1) + [(0, N_SMALL - small.shape[-1])]
        return big, jnp.pad(small, pad)

    w_big, w_small = regroup(w_in)
    w_big, w_small = w_big.astype(BF16), w_small.astype(BF16)
    b_big, b_small = regroup(b_in[:, None, :])
    wg_gate = jnp.pad(gla_w_gate, ((0, 0), (0, N_SMALL - GLA_LR), (0, 0))).astype(BF16)
    wup = ffn_w_up.astype(BF16)
    wdn = ffn_w_down.astype(BF16)
    wbg = w_branch_gla.astype(BF16)
    wbm = w_branch_ml.astype(BF16)
    wo = w_out.astype(BF16)

    for l in range(depth):
        big, la, mg, lamin = _inproj(h, norm_mix[l][None], w_big[l], b_big[l], w_small[l], b_small[l],
                                     wg_gate[l], gla_b_gate[l][None], ml_conv_w[l],
                                     ml_conv_b[l][None], tiles_per_batch)
        o_gla = _gla(big, la, lamin, gla_norm[l][None], tiles_per_batch)
        o_ml = _mlstm(big, mg, ml_norm[l][None], tiles_per_batch)
        h = _merge(o_gla, o_ml, big, h, wbg[l], wbm[l], wo[l])
        h = _ffn(h, norm_ffn[l][None], wup[l], ffn_conv_w[l], ffn_conv_b[l][None], wdn[l],
                 tiles_per_batch)

    return _final(h.reshape(b, lp, D_MODEL), norm_final[None], seq)
```

```python
import functools
import math

import jax
import jax.numpy as jnp
from jax import lax
from jax.experimental import pallas as pl
from jax.experimental.pallas import tpu as pltpu

F32 = jnp.float32
BF16 = jnp.bfloat16

D_MODEL = 1024
N_META = 16
CHUNK = 64
BLOCK = 64
GLA_H = 4
GLA_DK = 128
GLA_DV = 256
GLA_LR = 16
GLA_TAU = 16.0
ML_H = 4
ML_DK = 128
ML_DV = 256
ML_CONV = 4
D_FF = 2816
FFN_CONV = 3
EPS = 1e-6
LOG2E = math.log2(math.e)

GLA_QK = GLA_H * GLA_DK
GLA_V = GLA_H * GLA_DV
ML_QK = ML_H * ML_DK
ML_V = ML_H * ML_DV

TM = 832
CHUNKS_PER_TILE = TM // CHUNK

INPROJ_TN = 2048
N_BIG = 4 * INPROJ_TN
OFF_GQ, OFF_GK, OFF_GV = 0, 512, 1024
OFF_MQ, OFF_MK, OFF_MV = 2048, 2560, 3072
OFF_GR, OFF_MO = 4096, 5120
OFF_GA, OFF_GB = 6144, 7168
N_SMALL = 128
SMALL_I = 16
SMALL_F = 20

FF_TILE = 256
N_FF_TILES = D_FF // FF_TILE
HALO = 16
VMEM_LIMIT = 56 * 1024 * 1024
ML_CHUNKS = (320, 256, 256)
CONV_CARRY = 8 * (ML_CONV - 1)
GLA_SAFE_LOG2_DECAY = -3.0

NT_DIMS = (((1,), (1,)), ((), ()))
TN_DIMS = (((0,), (0,)), ((), ()))


def _rms(x, g):
    return x * lax.rsqrt(jnp.mean(x * x, axis=-1, keepdims=True) + EPS) * g


def _sigmoid(x):
    return 0.5 * jnp.tanh(0.5 * x) + 0.5


def _log2_sigmoid(x):
    x2 = x * LOG2E
    return jnp.minimum(x2, 0.0) - jnp.log2(1.0 + jnp.exp2(-jnp.abs(x2)))


def _row_time(r):
    blk = jnp.bitwise_and(r, -BLOCK)
    return blk + jnp.left_shift(jnp.bitwise_and(r, 7), 3) + jnp.bitwise_and(jnp.right_shift(r, 3), 7)


def _row_of_time(t):
    return (t // BLOCK) * BLOCK + (t % 8) * 8 + (t // 8) % 8


def _causal(n):
    t_r = _row_time(lax.broadcasted_iota(jnp.int32, (n, n), 0))
    t_c = _row_time(lax.broadcasted_iota(jnp.int32, (n, n), 1))
    return t_r >= t_c


def _block_swap(x):
    n, c = x.shape
    return jnp.swapaxes(x.reshape(n // BLOCK, 8, 8, c), 1, 2).reshape(n, c)


def _time_shift(x, k, prev_tail):
    n, c = x.shape
    x4 = x.reshape(n // BLOCK, 8, 8, c)
    tail = x4[:, 8 - k:]
    prev = jnp.concatenate([prev_tail.reshape(1, k, 8, c), tail[:-1]], axis=0)
    sub = lax.broadcasted_iota(jnp.int32, tail.shape, 2)
    wrapped = jnp.where(sub == 0, pltpu.roll(prev, 1, axis=2), pltpu.roll(tail, 1, axis=2))
    return jnp.concatenate([wrapped, x4[:, :8 - k]], axis=1).reshape(n, c)


def _cumsum_rows(tri_b, x):
    hi = x.astype(BF16)
    lo = (x - hi.astype(F32)).astype(BF16)
    return (jnp.dot(tri_b, hi, preferred_element_type=F32)
            + jnp.dot(tri_b, lo, preferred_element_type=F32))


def _heads(x, d):
    return jnp.stack([x[:, h * d:(h + 1) * d] for h in range(GLA_H)], axis=0)


def _params(sem):
    return pltpu.CompilerParams(dimension_semantics=sem, vmem_limit_bytes=VMEM_LIMIT)


def _skewed(n, phases):
    for i in range(n + len(phases) - 1):
        for k, phase in enumerate(phases):
            if 0 <= i - k < n:
                phase(i - k)


def _prep_kernel(seq, meta_ref, x_ref, xprev_ref, o_ref):
    j = pl.program_id(1)
    head = jnp.where(j == 0, meta_ref[...], xprev_ref[0])
    nat = jnp.concatenate([head, x_ref[0, :TM - N_META, :]], axis=0)
    row = j * TM + lax.broadcasted_iota(jnp.int32, (TM, 1), 0)
    nat = jnp.where(row < N_META + seq, nat, 0.0)
    o_ref[0] = _block_swap(nat)


def _prep(x, meta, lp):
    b, seq, _ = x.shape
    tiles_per_batch = lp // TM
    meta_blocks_per_tile = TM // N_META
    return pl.pallas_call(
        functools.partial(_prep_kernel, seq),
        grid=(b, tiles_per_batch),
        in_specs=[
            pl.BlockSpec((N_META, D_MODEL), lambda i, j: (0, 0)),
            pl.BlockSpec((1, TM, D_MODEL), lambda i, j: (i, j, 0)),
            pl.BlockSpec((1, N_META, D_MODEL),
                         lambda i, j: (i, jnp.maximum(j * meta_blocks_per_tile - 1, 0), 0)),
        ],
        out_specs=pl.BlockSpec((1, TM, D_MODEL), lambda i, j: (i * tiles_per_batch + j, 0, 0)),
        out_shape=jax.ShapeDtypeStruct((b * tiles_per_batch, TM, D_MODEL), F32),
        compiler_params=_params(("arbitrary", "arbitrary")),
        name="prep",
    )(meta, x, x)


def _inproj_kernel(tiles_per_batch, h_ref, g_ref, w_ref, b_ref, ws_ref, bs_ref, wg_ref, bg_ref,
                   cw_ref, cb_ref, big_ref, la_ref, mg_ref, lamin_ref, xn_ref, carry_ref):
    j = pl.program_id(1)

    def project():
        return jnp.dot(xn_ref[...], w_ref[...], preferred_element_type=F32) + b_ref[...]

    @pl.when(j == 0)
    def _():
        xn = _rms(h_ref[0], g_ref[...]).astype(BF16)
        xn_ref[...] = xn
        small = jnp.dot(xn, ws_ref[...], preferred_element_type=F32) + bs_ref[...]
        x = jnp.dot(small.astype(BF16), wg_ref[...], preferred_element_type=F32) + bg_ref[...]
        la = _log2_sigmoid(x) * (1.0 / GLA_TAU)
        la_ref[0] = la
        lmin = jnp.min(la.reshape(TM // 8, 8, GLA_QK), axis=0)
        lamin_ref[0] = functools.reduce(
            jnp.minimum, [lmin[:, i * 128:(i + 1) * 128] for i in range(GLA_QK // 128)])
        lane = lax.broadcasted_iota(jnp.int32, small.shape, 1)
        is_f = jnp.logical_and(lane >= SMALL_F, lane < SMALL_F + ML_H)
        mg_ref[0] = jnp.where(is_f, _log2_sigmoid(small), small * LOG2E)
        acc = project()
        big_ref[0, :, :GLA_QK] = (acc[:, :GLA_QK] * (GLA_DK ** -0.5)).astype(BF16)
        big_ref[0, :, GLA_QK:] = acc[:, GLA_QK:].astype(BF16)

    @pl.when(j == 1)
    def _():
        @pl.when(pl.program_id(0) % tiles_per_batch == 0)
        def _():
            carry_ref[...] = jnp.zeros_like(carry_ref)

        acc = project()
        pre = acc[:, :2 * ML_QK]
        y = cb_ref[...] + cw_ref[ML_CONV - 1:ML_CONV, :] * pre
        for tap in range(ML_CONV - 1):
            k = ML_CONV - 1 - tap
            y = y + cw_ref[tap:tap + 1, :] * _time_shift(pre, k, carry_ref[CONV_CARRY - 8 * k:, :])
        carry_ref[...] = pre[TM - CONV_CARRY:, :]
        qk = y * _sigmoid(y)
        big_ref[0, :, :ML_QK] = (qk[:, :ML_QK] * (ML_DK ** -0.5)).astype(BF16)
        big_ref[0, :, ML_QK:2 * ML_QK] = qk[:, ML_QK:].astype(BF16)
        big_ref[0, :, 2 * ML_QK:] = acc[:, 2 * ML_QK:].astype(BF16)

    @pl.when(j == 2)
    def _():
        acc = project()
        r = acc[:, :GLA_V]
        big_ref[0, :, :GLA_V] = (r * _sigmoid(r)).astype(BF16)
        big_ref[0, :, GLA_V:] = _sigmoid(acc[:, GLA_V:]).astype(BF16)

    @pl.when(j == 3)
    def _():
        big_ref[0] = _sigmoid(project()).astype(BF16)


def _inproj(h, g, w, b, ws, bs, wg, bg, cw, cb, tiles_per_batch, layer):
    r = h.shape[0]
    const = lambda i, j: (0, 0)
    return pl.pallas_call(
        functools.partial(_inproj_kernel, tiles_per_batch),
        grid=(r, N_BIG // INPROJ_TN),
        in_specs=[
            pl.BlockSpec((1, TM, D_MODEL), lambda i, j: (i, 0, 0)),
            pl.BlockSpec((1, D_MODEL), const),
            pl.BlockSpec((None, D_MODEL, INPROJ_TN), lambda i, j: (layer, 0, j)),
            pl.BlockSpec((None, 1, INPROJ_TN), lambda i, j: (layer, 0, j)),
            pl.BlockSpec((None, D_MODEL, N_SMALL), lambda i, j: (layer, 0, 0)),
            pl.BlockSpec((None, 1, N_SMALL), lambda i, j: (layer, 0, 0)),
            pl.BlockSpec((None, N_SMALL, GLA_QK), lambda i, j: (layer, 0, 0)),
            pl.BlockSpec((1, GLA_QK), const),
            pl.BlockSpec((ML_CONV, 2 * ML_QK), const),
            pl.BlockSpec((1, 2 * ML_QK), const),
        ],
        out_specs=[
            pl.BlockSpec((1, TM, INPROJ_TN), lambda i, j: (i, 0, j)),
            pl.BlockSpec((1, TM, GLA_QK), lambda i, j: (i, 0, 0)),
            pl.BlockSpec((1, TM, N_SMALL), lambda i, j: (i, 0, 0)),
            pl.BlockSpec((1, 8, 128), lambda i, j: (i, 0, 0)),
        ],
        out_shape=[
            jax.ShapeDtypeStruct((r, TM, N_BIG), BF16),
            jax.ShapeDtypeStruct((r, TM, GLA_QK), F32),
            jax.ShapeDtypeStruct((r, TM, N_SMALL), F32),
            jax.ShapeDtypeStruct((r, 8, 128), F32),
        ],
        scratch_shapes=[
            pltpu.VMEM((TM, D_MODEL), BF16),
            pltpu.VMEM((CONV_CARRY, 2 * ML_QK), F32),
        ],
        compiler_params=_params(("arbitrary", "arbitrary")),
        name="inproj",
    )(h, g, w, b, ws, bs, wg, bg, cw, cb)


def _gla_kernel(tiles_per_batch, q_ref, k_ref, v_ref, r_ref, la_ref, lamin_ref, gn_ref,
                o_ref, s_ref, qs_ref, qd_ref, kd_ref, ks_ref, eend_ref, oi_ref, kv_ref, bc_ref, qf_ref):
    @pl.when(pl.program_id(0) % tiles_per_batch == 0)
    def _():
        s_ref[...] = jnp.zeros_like(s_ref)

    safe = jnp.min(lamin_ref[0]) >= GLA_SAFE_LOG2_DECAY

    @pl.when(safe)
    def _():
        _gla_factored(q_ref, k_ref, v_ref, r_ref, la_ref, gn_ref, o_ref, s_ref, qs_ref, qd_ref,
                      kd_ref, ks_ref, eend_ref, oi_ref, kv_ref)

    @pl.when(jnp.logical_not(safe))
    def _():
        _gla_direct(q_ref, k_ref, v_ref, r_ref, la_ref, gn_ref, o_ref, s_ref, oi_ref, bc_ref, qf_ref)


def _gla_direct(q_ref, k_ref, v_ref, r_ref, la_ref, gn_ref, o_ref, s_ref, oi_ref, bc_ref, qf_ref):
    tri_b = _causal(CHUNK).astype(BF16)
    s_time = _row_time(lax.broadcasted_iota(jnp.int32, (CHUNK, 1), 0))

    def chunk(c, carry):
        r0 = pl.multiple_of(c * CHUNK, CHUNK)
        rows = pl.ds(r0, CHUNK)
        bc = _cumsum_rows(tri_b, la_ref[0, rows, :])
        bend = bc[CHUNK - 1:CHUNK, :]
        q = q_ref[0, rows, :].astype(F32)
        k = k_ref[0, rows, :].astype(F32)
        bc_ref[...] = bc
        qf_ref[...] = q
        qs = (q * jnp.exp2(bc)).astype(BF16)
        ks = (k * jnp.exp2(bend - bc)).astype(BF16)
        e_end = jnp.exp2(bend)
        for h in range(GLA_H):
            ksl = slice(h * GLA_DK, (h + 1) * GLA_DK)
            vsl = slice(h * GLA_DV, (h + 1) * GLA_DV)
            kh = k[:, ksl]
            bch = bc[:, ksl]
            vh = v_ref[0, rows, vsl].astype(F32)

            def group(g, carry2):
                g8 = pl.multiple_of(g * 8, 8)
                bt8 = bc_ref[pl.ds(g8, 8), ksl]
                qt8 = qf_ref[pl.ds(g8, 8), ksl]
                out_rows = []
                for j in range(8):
                    e = jnp.exp2(jnp.minimum(bt8[j:j + 1, :] - bch, 0.0))
                    col = jnp.sum(e * kh * qt8[j:j + 1, :], axis=-1, keepdims=True)
                    col = jnp.where(s_time <= 8 * j + g, col, 0.0)
                    out_rows.append(jnp.sum(col * vh, axis=0, keepdims=True))
                oi_ref[pl.ds(r0 + g8, 8), vsl] = jnp.concatenate(out_rows, axis=0)
                return carry2

            lax.fori_loop(0, CHUNK // 8, group, 0)
            st = s_ref[h]
            o = oi_ref[rows, vsl] + lax.dot_general(qs[:, ksl], st.astype(BF16), NT_DIMS,
                                                    preferred_element_type=F32)
            s_ref[h] = st * e_end[:, ksl] + lax.dot_general(
                v_ref[0, rows, vsl], ks[:, ksl], TN_DIMS, preferred_element_type=F32)
            o = _rms(o, gn_ref[...]) * r_ref[0, rows, vsl].astype(F32)
            o_ref[0, rows, vsl] = o.astype(BF16)
        return carry

    lax.fori_loop(0, CHUNKS_PER_TILE, chunk, 0)


def _gla_factored(q_ref, k_ref, v_ref, r_ref, la_ref, gn_ref, o_ref, s_ref, qs_ref, qd_ref,
                  kd_ref, ks_ref, eend_ref, oi_ref, kv_ref):
    tri = _causal(CHUNK)
    tri_b = tri.astype(BF16)
    mid_row = _row_of_time(CHUNK // 2 - 1)

    def phase1(c):
        rows = slice(c * CHUNK, (c + 1) * CHUNK)
        bc = _cumsum_rows(tri_b, la_ref[0, rows, :])
        ref = bc[mid_row:mid_row + 1, :]
        bend = bc[CHUNK - 1:CHUNK, :]
        e_fwd = jnp.exp2(bc - ref)
        e_bwd = jnp.exp2(ref - bc)
        q = q_ref[0, rows, :]
        k = k_ref[0, rows, :]
        qd_ref[rows, :] = q * e_fwd.astype(BF16)
        qs_ref[rows, :] = q * (e_fwd * jnp.exp2(ref)).astype(BF16)
        kd_ref[rows, :] = k * e_bwd.astype(BF16)
        ks_ref[rows, :] = k * (e_bwd * jnp.exp2(bend - ref)).astype(BF16)
        eend_ref[c:c + 1, :] = jnp.exp2(bend)

    BQK = (((2,), (2,)), ((0,), (0,)))
    BPV = (((2,), (1,)), ((0,), (0,)))
    BTN = (((1,), (1,)), ((0,), (0,)))

    def phase2(c):
        rows = slice(c * CHUNK, (c + 1) * CHUNK)
        v = _heads(v_ref[0, rows, :], GLA_DV)
        sc = lax.dot_general(_heads(qd_ref[rows, :], GLA_DK), _heads(kd_ref[rows, :], GLA_DK), BQK,
                             preferred_element_type=F32)
        sc = jnp.where(tri[None], sc, 0.0).astype(BF16)
        oi = lax.dot_general(sc, v, BPV, preferred_element_type=F32)
        for h in range(GLA_H):
            oi_ref[rows, h * GLA_DV:(h + 1) * GLA_DV] = oi[h]
        kv_ref[c] = lax.dot_general(v, _heads(ks_ref[rows, :], GLA_DK), BTN, preferred_element_type=F32)

    def phase3(c):
        rows = slice(c * CHUNK, (c + 1) * CHUNK)
        st = s_ref[...]
        o = _heads(oi_ref[rows, :], GLA_DV) + lax.dot_general(
            _heads(qs_ref[rows, :], GLA_DK), st.astype(BF16), BQK, preferred_element_type=F32)
        s_ref[...] = st * _heads(eend_ref[c:c + 1, :], GLA_DK) + kv_ref[c]
        o = o * lax.rsqrt(jnp.mean(o * o, axis=-1, keepdims=True) + EPS)
        for h in range(GLA_H):
            vsl = slice(h * GLA_DV, (h + 1) * GLA_DV)
            o_ref[0, rows, vsl] = (o[h] * gn_ref[...] * r_ref[0, rows, vsl].astype(F32)).astype(BF16)

    _skewed(CHUNKS_PER_TILE, (phase1, phase2, phase3))


def _gla(big, la, lamin, gn, tiles_per_batch):
    r = big.shape[0]
    return pl.pallas_call(
        functools.partial(_gla_kernel, tiles_per_batch),
        grid=(r,),
        in_specs=[
            pl.BlockSpec((1, TM, GLA_QK), lambda i: (i, 0, OFF_GQ // GLA_QK)),
            pl.BlockSpec((1, TM, GLA_QK), lambda i: (i, 0, OFF_GK // GLA_QK)),
            pl.BlockSpec((1, TM, GLA_V), lambda i: (i, 0, OFF_GV // GLA_V)),
            pl.BlockSpec((1, TM, GLA_V), lambda i: (i, 0, OFF_GR // GLA_V)),
            pl.BlockSpec((1, TM, GLA_QK), lambda i: (i, 0, 0)),
            pl.BlockSpec((1, 8, 128), lambda i: (i, 0, 0)),
            pl.BlockSpec((1, GLA_DV), lambda i: (0, 0)),
        ],
        out_specs=pl.BlockSpec((1, TM, GLA_V), lambda i: (i, 0, 0)),
        out_shape=jax.ShapeDtypeStruct((r, TM, GLA_V), BF16),
        scratch_shapes=[
            pltpu.VMEM((GLA_H, GLA_DV, GLA_DK), F32),
            pltpu.VMEM((TM, GLA_QK), BF16),
            pltpu.VMEM((TM, GLA_QK), BF16),
            pltpu.VMEM((TM, GLA_QK), BF16),
            pltpu.VMEM((TM, GLA_QK), BF16),
            pltpu.VMEM((16, GLA_QK), F32),
            pltpu.VMEM((TM, GLA_V), F32),
            pltpu.VMEM((CHUNKS_PER_TILE, GLA_H, GLA_DV, GLA_DK), F32),
            pltpu.VMEM((CHUNK, GLA_QK), F32),
            pltpu.VMEM((CHUNK, GLA_QK), F32),
        ],
        compiler_params=_params(("arbitrary",)),
        name="gla",
    )(big, big, big, big, la, lamin, gn)


def _mlstm_kernel(tiles_per_batch, q_ref, k_ref, v_ref, og_ref, mg_ref, mn_ref, o_ref,
                  c_ref, n_ref, m_ref):
    @pl.when(pl.program_id(0) % tiles_per_batch == 0)
    def _():
        c_ref[...] = jnp.zeros_like(c_ref)
        n_ref[...] = jnp.zeros_like(n_ref)
        m_ref[...] = jnp.zeros_like(m_ref)

    BQK = (((2,), (2,)), ((0,), (0,)))
    BPV = (((2,), (1,)), ((0,), (0,)))
    BTN = (((1,), (1,)), ((0,), (0,)))

    def lanes(x, n):
        reps = -(-n // 128)
        return jnp.concatenate([x] * reps, axis=-1)[:, :, :n]

    def chunk(r0, cs):
        tri = _causal(cs)
        tri_b = tri.astype(BF16)
        rows = slice(r0, r0 + cs)
        sm = mg_ref[0, rows, :]
        fc = _cumsum_rows(tri_b, sm)
        fct = fc.T
        smt = sm.T
        f_col = jnp.stack([fc[:, SMALL_F + h:SMALL_F + h + 1] for h in range(ML_H)], axis=0)
        li_col = jnp.stack([sm[:, SMALL_I + h:SMALL_I + h + 1] for h in range(ML_H)], axis=0)
        f_row = jnp.stack([fct[SMALL_F + h:SMALL_F + h + 1, :] for h in range(ML_H)], axis=0)
        li_row = jnp.stack([smt[SMALL_I + h:SMALL_I + h + 1, :] for h in range(ML_H)], axis=0)
        m_prev = m_ref[:, 0:1, :]
        f_rep = f_col + jnp.zeros((ML_H, cs, 128), F32)
        d_intra = jnp.where(tri[None], lanes(f_rep, cs) - f_row + li_row, -jnp.inf)
        d_inter = f_rep + m_prev
        a_rep = jnp.max(d_intra, axis=-1, keepdims=True) + jnp.zeros((ML_H, cs, 128), F32)
        m = jnp.maximum(d_inter, a_rep)
        w_intra = jnp.exp2(d_intra - lanes(m, cs))
        w_inter = jnp.exp2(d_inter - m)
        qh = _heads(q_ref[0, rows, :], ML_DK)
        kh = _heads(k_ref[0, rows, :], ML_DK)
        v = _heads(v_ref[0, rows, :], ML_DV)
        ct = c_ref[...]
        n_prev = n_ref[:, 0:1, :]
        qkw = (lax.dot_general(qh, kh, BQK, preferred_element_type=F32) * w_intra).astype(BF16)
        num = lanes(w_inter, ML_DV) * lax.dot_general(qh, ct.astype(BF16), BQK, preferred_element_type=F32)
        num = num + lax.dot_general(qkw, v, BPV, preferred_element_type=F32)
        n_rows = jnp.concatenate([n_ref[...]] * (128 // 8), axis=1).astype(BF16)
        qn = lax.dot_general(qh, n_rows, BQK, preferred_element_type=F32)
        rowsum = lax.dot_general(qkw, jnp.ones((ML_H, cs, 128), BF16), BPV, preferred_element_type=F32)
        den = w_inter * qn + rowsum
        hh = num * lanes(1.0 / jnp.maximum(jnp.abs(den), jnp.exp2(-m)), ML_DV)
        f_last = f_col[:, cs - 1:cs, :]
        d_state = f_last - f_col + li_col
        m_new = jnp.maximum(f_last + m_prev, jnp.max(d_state, axis=1, keepdims=True))
        g_prev = jnp.exp2(f_last + m_prev - m_new)
        kw = kh.astype(F32) * jnp.exp2((d_state + jnp.zeros((ML_H, cs, 128), F32)) - m_new)
        c_ref[...] = g_prev * ct + lax.dot_general(v, kw.astype(BF16), BTN, preferred_element_type=F32)
        n_ref[...] = jnp.broadcast_to(g_prev * n_prev + jnp.sum(kw, axis=1, keepdims=True),
                                      (ML_H, 8, ML_DK))
        m_ref[...] = jnp.broadcast_to(m_new, (ML_H, 8, 128))
        ms = lax.dot_general((hh * hh).astype(BF16), jnp.ones((ML_H, ML_DV, 128), BF16), BPV,
                             preferred_element_type=F32) * (1.0 / ML_DV)
        hh = hh * lanes(lax.rsqrt(ms + EPS), ML_DV)
        for h in range(ML_H):
            vsl = slice(h * ML_DV, (h + 1) * ML_DV)
            o_ref[0, rows, vsl] = (hh[h] * mn_ref[:, vsl] * og_ref[0, rows, vsl].astype(F32)).astype(BF16)

    r0 = 0
    for cs in ML_CHUNKS:
        chunk(r0, cs)
        r0 += cs


def _mlstm(big, mg, mn, tiles_per_batch):
    r = big.shape[0]
    return pl.pallas_call(
        functools.partial(_mlstm_kernel, tiles_per_batch),
        grid=(r,),
        in_specs=[
            pl.BlockSpec((1, TM, ML_QK), lambda i: (i, 0, OFF_MQ // ML_QK)),
            pl.BlockSpec((1, TM, ML_QK), lambda i: (i, 0, OFF_MK // ML_QK)),
            pl.BlockSpec((1, TM, ML_V), lambda i: (i, 0, OFF_MV // ML_V)),
            pl.BlockSpec((1, TM, ML_V), lambda i: (i, 0, OFF_MO // ML_V)),
            pl.BlockSpec((1, TM, N_SMALL), lambda i: (i, 0, 0)),
            pl.BlockSpec((1, ML_V), lambda i: (0, 0)),
        ],
        out_specs=pl.BlockSpec((1, TM, ML_V), lambda i: (i, 0, 0)),
        out_shape=jax.ShapeDtypeStruct((r, TM, ML_V), BF16),
        scratch_shapes=[
            pltpu.VMEM((ML_H, ML_DV, ML_DK), F32),
            pltpu.VMEM((ML_H, 8, ML_DK), F32),
            pltpu.VMEM((ML_H, 8, 128), F32),
        ],
        compiler_params=_params(("arbitrary",)),
        name="mlstm",
    )(big, big, big, big, mg, mn)


def _merge_kernel(og_ref, om_ref, ga_ref, gb_ref, h_ref, wg_ref, wm_ref, wo_ref, out_ref):
    a = jnp.dot(og_ref[0], wg_ref[...], preferred_element_type=F32)
    b = jnp.dot(om_ref[0], wm_ref[...], preferred_element_type=F32)
    y = ga_ref[0].astype(F32) * a + gb_ref[0].astype(F32) * b
    out_ref[0] = h_ref[0] + jnp.dot(y.astype(BF16), wo_ref[...], preferred_element_type=F32)


def _merge(o_gla, o_ml, big, h, wg, wm, wo, layer):
    r = h.shape[0]
    wspec = pl.BlockSpec((None, D_MODEL, D_MODEL), lambda i: (layer, 0, 0))
    return pl.pallas_call(
        _merge_kernel,
        grid=(r,),
        in_specs=[
            pl.BlockSpec((1, TM, GLA_V), lambda i: (i, 0, 0)),
            pl.BlockSpec((1, TM, ML_V), lambda i: (i, 0, 0)),
            pl.BlockSpec((1, TM, D_MODEL), lambda i: (i, 0, OFF_GA // D_MODEL)),
            pl.BlockSpec((1, TM, D_MODEL), lambda i: (i, 0, OFF_GB // D_MODEL)),
            pl.BlockSpec((1, TM, D_MODEL), lambda i: (i, 0, 0)),
            wspec, wspec, wspec,
        ],
        out_specs=pl.BlockSpec((1, TM, D_MODEL), lambda i: (i, 0, 0)),
        out_shape=jax.ShapeDtypeStruct(h.shape, F32),
        input_output_aliases={4: 0},
        compiler_params=_params(("arbitrary",)),
        name="merge",
    )(o_gla, o_ml, big, big, h, wg, wm, wo)


def _ffn_kernel(tiles_per_batch, h_ref, halo_ref, g_ref, wup_ref, cw_ref, cb_ref, wdn_ref,
                out_ref, xn_ref, a_ref):
    keep = (pl.program_id(0) % tiles_per_batch != 0).astype(F32)
    xn_ref[0:HALO, :] = (_rms(halo_ref[0], g_ref[...]) * keep).astype(BF16)
    xn_ref[HALO:HALO + TM, :] = _rms(h_ref[0], g_ref[...]).astype(BF16)

    def conv_up(cols):
        u = jnp.dot(xn_ref[...], wup_ref[:, cols], preferred_element_type=F32)
        cur = u[HALO:, :]
        y = cb_ref[:, cols] + cw_ref[FFN_CONV - 1:FFN_CONV, cols] * cur
        for tap in range(FFN_CONV - 1):
            k = FFN_CONV - 1 - tap
            y = y + cw_ref[tap:tap + 1, cols] * _time_shift(cur, k, u[HALO - 8 * k:HALO, :])
        return y

    for j in range(N_FF_TILES):
        gate = conv_up(slice(j * FF_TILE, (j + 1) * FF_TILE))
        up = conv_up(slice(D_FF + j * FF_TILE, D_FF + (j + 1) * FF_TILE))
        a_ref[:, j * FF_TILE:(j + 1) * FF_TILE] = (gate * _sigmoid(gate) * up).astype(BF16)

    out_ref[0] = h_ref[0] + jnp.dot(a_ref[...], wdn_ref[...], preferred_element_type=F32)


def _ffn(h, g, wup, cw, cb, wdn, tiles_per_batch, layer):
    r = h.shape[0]
    last_halo_block = TM // HALO - 1
    const = lambda i: (0, 0)
    resident = pl.Buffered(1)
    return pl.pallas_call(
        functools.partial(_ffn_kernel, tiles_per_batch),
        grid=(r,),
        in_specs=[
            pl.BlockSpec((1, TM, D_MODEL), lambda i: (i, 0, 0)),
            pl.BlockSpec((1, HALO, D_MODEL), lambda i: (jnp.maximum(i - 1, 0), last_halo_block, 0)),
            pl.BlockSpec((1, D_MODEL), const),
            pl.BlockSpec((None, D_MODEL, 2 * D_FF), lambda i: (layer, 0, 0), pipeline_mode=resident),
            pl.BlockSpec((FFN_CONV, 2 * D_FF), const),
            pl.BlockSpec((1, 2 * D_FF), const),
            pl.BlockSpec((None, D_FF, D_MODEL), lambda i: (layer, 0, 0), pipeline_mode=resident),
        ],
        out_specs=pl.BlockSpec((1, TM, D_MODEL), lambda i: (i, 0, 0)),
        out_shape=jax.ShapeDtypeStruct(h.shape, F32),
        scratch_shapes=[
            pltpu.VMEM((HALO + TM, D_MODEL), BF16),
            pltpu.VMEM((TM, D_FF), BF16),
        ],
        compiler_params=_params(("arbitrary",)),
        name="ffn",
    )(h, h, g, wup, cw, cb, wdn)


def _final_kernel(a_ref, nxt_ref, g_ref, o_ref):
    y = jnp.concatenate([_block_swap(a_ref[0])[N_META:, :], _block_swap(nxt_ref[0])[:N_META, :]], axis=0)
    o_ref[0] = _rms(y, g_ref[...])


def _final(h, g, seq):
    b, lp, _ = h.shape
    blocks_per_tile = TM // BLOCK
    last_block = lp // BLOCK - 1
    return pl.pallas_call(
        _final_kernel,
        grid=(b, lp // TM),
        in_specs=[
            pl.BlockSpec((1, TM, D_MODEL), lambda i, j: (i, j, 0)),
            pl.BlockSpec((1, BLOCK, D_MODEL),
                         lambda i, j: (i, jnp.minimum((j + 1) * blocks_per_tile, last_block), 0)),
            pl.BlockSpec((1, D_MODEL), lambda i, j: (0, 0)),
        ],
        out_specs=pl.BlockSpec((1, TM, D_MODEL), lambda i, j: (i, j, 0)),
        out_shape=jax.ShapeDtypeStruct((b, seq, D_MODEL), F32),
        compiler_params=_params(("arbitrary", "arbitrary")),
        name="final_norm",
    )(h, h, g)


def kernel(x, meta, norm_mix, w_in, b_in, gla_w_gate, gla_b_gate, gla_norm, ml_conv_w, ml_conv_b,
           ml_norm, w_branch_gla, w_branch_ml, w_out, norm_ffn, ffn_w_up, ffn_conv_w, ffn_conv_b,
           ffn_w_down, norm_final):
    b, seq, _ = x.shape
    depth = w_in.shape[0]
    lp = -(-(N_META + seq) // TM) * TM
    tiles_per_batch = lp // TM
    assert lp % CHUNK == 0

    h = _prep(x, meta.astype(x.dtype), lp)

    c_gr = 2 * GLA_QK + GLA_V
    c_lr = c_gr + GLA_V
    c_mq = c_lr + GLA_LR
    c_mo = c_mq + 2 * ML_QK + ML_V
    c_mi = c_mo + ML_V
    c_ga = c_mi + 2 * ML_H

    def regroup(t):
        big = jnp.concatenate([t[..., :c_gr], t[..., c_mq:c_mo], t[..., c_gr:c_lr],
                               t[..., c_mo:c_mi], t[..., c_ga:]], axis=-1)
        small = jnp.concatenate([t[..., c_lr:c_mq], t[..., c_mi:c_ga]], axis=-1)
        pad = [(0, 0)] * (t.ndim - 1) + [(0, N_SMALL - small.shape[-1])]
        return big, jnp.pad(small, pad)

    w_big, w_small = regroup(w_in)
    w_big, w_small = w_big.astype(BF16), w_small.astype(BF16)
    b_big, b_small = regroup(b_in[:, None, :])
    wg_gate = jnp.pad(gla_w_gate, ((0, 0), (0, N_SMALL - GLA_LR), (0, 0))).astype(BF16)
    wup = ffn_w_up.astype(BF16)
    wdn = ffn_w_down.astype(BF16)
    wbg = w_branch_gla.astype(BF16)
    wbm = w_branch_ml.astype(BF16)
    wo = w_out.astype(BF16)

    for l in range(depth):
        big, la, mg, lamin = _inproj(h, norm_mix[l][None], w_big, b_big, w_small, b_small,
                                     wg_gate, gla_b_gate[l][None], ml_conv_w[l],
                                     ml_conv_b[l][None], tiles_per_batch, l)
        o_gla = _gla(big, la, lamin, gla_norm[l][None], tiles_per_batch)
        o_ml = _mlstm(big, mg, ml_norm[l][None], tiles_per_batch)
        h = _merge(o_gla, o_ml, big, h, wbg, wbm, wo, l)
        h = _ffn(h, norm_ffn[l][None], wup, ffn_conv_w[l], ffn_conv_b[l][None], wdn,
                 tiles_per_batch, l)

    return _final(h.reshape(b, lp, D_MODEL), norm_final[None], seq)
```
